```python
import jax, jax.numpy as jnp
from jax import lax
import numpy as np

D_MODEL = 1024
BATCH = 4
SEQ = 4096
DEPTH = 4
DEC_BATCH = 32
DEC_SEQ = 16
PAST_LEN = 1024

CHUNK = 64
SGU_CHUNK = 128
SGU_WIDTH = D_MODEL // 2
SGU_GROUPS = 4
SGU_GROUP_DIM = SGU_WIDTH // SGU_GROUPS
SB_HEADS = 8
SB_HEAD_DIM = 64
SB_WIDTH = SB_HEADS * SB_HEAD_DIM
MIX_WIDTH = SGU_WIDTH + SB_WIDTH
IN_WIDTH = 2 * SGU_WIDTH + 3 * SB_WIDTH
SB_Q_BLOCK = 128
N_MEM = 256
MEM_HEADS = 4
MEM_HEAD_DIM = D_MODEL // MEM_HEADS
D_FF = -(-8 * D_MODEL // (3 * 256)) * 256
EPS = 1e-6

kernel_name = 'hybrid_sgu_stickbreak_stream_step'


def _rmsnorm(x, g):
    xf = x.astype(jnp.float32)
    y = xf * lax.rsqrt(jnp.mean(xf * xf, axis=-1, keepdims=True) + EPS)
    return (y * g.astype(jnp.float32)).astype(x.dtype)


def _mixer_inputs(h, w_in, g_v):
    B, T = h.shape[0], h.shape[1]
    proj = h @ w_in
    u, va, q, k, vb = jnp.split(
        proj, [SGU_WIDTH, 2 * SGU_WIDTH, 2 * SGU_WIDTH + SB_WIDTH, 2 * SGU_WIDTH + 2 * SB_WIDTH], axis=-1)
    u = jax.nn.gelu(u)
    va = _rmsnorm(jax.nn.gelu(va).reshape(B, T, SGU_GROUPS, SGU_GROUP_DIM), g_v)
    q = q.reshape(B, T, SB_HEADS, SB_HEAD_DIM)
    k = k.reshape(B, T, SB_HEADS, SB_HEAD_DIM)
    vb = vb.reshape(B, T, SB_HEADS, SB_HEAD_DIM)
    return u, va, q, k, vb


def _sgu_mix(v, w_s, b_s):
    L = v.shape[2]
    w = jnp.tril(w_s[:, :L, :L])
    bias = jnp.transpose(b_s[:, :L])[:, :, None]
    return jnp.einsum('gts,bcsge->bctge', w, v) + bias


def _stick_breaking(q, k, v, q_pos, k_pos):
    z = jnp.einsum('bqhd,bkhd->bhqk', q.astype(jnp.float32), k.astype(jnp.float32)) * (SB_HEAD_DIM ** -0.5)
    mask = k_pos[None, :] < q_pos[:, None]
    log_keep = jnp.where(mask, jax.nn.log_sigmoid(-z), 0.0)
    between = lax.cumsum(log_keep, axis=3, reverse=True) - log_keep
    a = jnp.where(mask, jnp.exp(jax.nn.log_sigmoid(z) + between), 0.0)
    o = jnp.einsum('bhqk,bkhd->bqhd', a, v.astype(jnp.float32))
    return o.astype(q.dtype)


def _sb_prompt(q, k, v):
    T = q.shape[1]
    pos = jnp.arange(T, dtype=jnp.int32)

    def blk(i):
        start = i * SB_Q_BLOCK
        qb = lax.dynamic_slice_in_dim(q, start, SB_Q_BLOCK, axis=1)
        pb = lax.dynamic_slice_in_dim(pos, start, SB_Q_BLOCK)
        return _stick_breaking(qb, k, v, pb, pos)

    out = lax.map(blk, jnp.arange(T // SB_Q_BLOCK, dtype=jnp.int32))
    return jnp.moveaxis(out, 0, 1).reshape(q.shape)


def _merge(a_out, sb_out, g_a, g_b, w_out):
    B, T = a_out.shape[0], a_out.shape[1]
    cat = jnp.concatenate([_rmsnorm(a_out, g_a), _rmsnorm(sb_out.reshape(B, T, SB_WIDTH), g_b)], axis=-1)
    return cat @ w_out


def _mem_kv(mem, w_mk, w_mv):
    B = mem.shape[0]
    mk = (mem @ w_mk).reshape(B, N_MEM, MEM_HEADS, MEM_HEAD_DIM)
    mv = (mem @ w_mv).reshape(B, N_MEM, MEM_HEADS, MEM_HEAD_DIM)
    return mk, mv


def _mem_attend(h, mk, mv, w_mq, w_mo):
    B, T = h.shape[0], h.shape[1]
    q = (h @ w_mq).reshape(B, T, MEM_HEADS, MEM_HEAD_DIM)
    s = jnp.einsum('bqhd,bkhd->bhqk', q.astype(jnp.float32), mk.astype(jnp.float32)) * (MEM_HEAD_DIM ** -0.5)
    p = jax.nn.softmax(s, axis=-1)
    o = jnp.einsum('bhqk,bkhd->bqhd', p, mv.astype(jnp.float32)).astype(h.dtype)
    return o.reshape(B, T, D_MODEL) @ w_mo


def _swiglu(h, w_gate, w_up, w_down):
    return (jax.nn.silu(h @ w_gate) * (h @ w_up)) @ w_down


def setup_inputs(seed: int = 0) -> dict:
    key = jax.random.key(seed)
    ks = jax.random.split(key, 32)
    f32 = jnp.float32

    def nrm(k, shape, scale=1.0):
        return jax.random.normal(k, shape, f32) * scale

    def gain(k, shape):
        return 1.0 + 0.05 * jax.random.normal(k, shape, f32)

    L = DEPTH
    return {
        'x_prompt': nrm(ks[0], (BATCH, SEQ, D_MODEL)),
        'x_sample': nrm(ks[1], (DEC_BATCH, DEC_SEQ, D_MODEL)),
        'cache_sb_k': nrm(ks[2], (L, DEC_BATCH, PAST_LEN, SB_HEADS, SB_HEAD_DIM)),
        'cache_sb_v': nrm(ks[3], (L, DEC_BATCH, PAST_LEN, SB_HEADS, SB_HEAD_DIM)),
        'cache_mem_k': nrm(ks[4], (L, DEC_BATCH, N_MEM, MEM_HEADS, MEM_HEAD_DIM)),
        'cache_mem_v': nrm(ks[5], (L, DEC_BATCH, N_MEM, MEM_HEADS, MEM_HEAD_DIM)),
        'mem_prompt': nrm(ks[6], (BATCH, N_MEM, D_MODEL)),
        'ln_mix': gain(ks[7], (L, D_MODEL)),
        'w_in': nrm(ks[8], (L, D_MODEL, IN_WIDTH), D_MODEL ** -0.5),
        'g_sgu_v': gain(ks[9], (L, SGU_GROUPS, SGU_GROUP_DIM)),
        'w_sgu': nrm(ks[10], (L, SGU_GROUPS, SGU_CHUNK, SGU_CHUNK), SGU_CHUNK ** -0.5),
        'b_sgu': 1.0 + 0.01 * jax.random.normal(ks[11], (L, SGU_GROUPS, SGU_CHUNK), f32),
        'g_out_sgu': gain(ks[12], (L, SGU_WIDTH)),
        'g_out_sb': gain(ks[13], (L, SB_WIDTH)),
        'w_out': nrm(ks[14], (L, MIX_WIDTH, D_MODEL), MIX_WIDTH ** -0.5),
        'ln_mem': gain(ks[15], (L, D_MODEL)),
        'w_mq': nrm(ks[16], (L, D_MODEL, D_MODEL), D_MODEL ** -0.5),
        'w_mk': nrm(ks[17], (L, D_MODEL, D_MODEL), D_MODEL ** -0.5),
        'w_mv': nrm(ks[18], (L, D_MODEL, D_MODEL), D_MODEL ** -0.5),
        'w_mo': nrm(ks[19], (L, D_MODEL, D_MODEL), D_MODEL ** -0.5),
        'ln_ffn': gain(ks[20], (L, D_MODEL)),
        'w_ffn_gate': nrm(ks[21], (L, D_MODEL, D_FF), D_MODEL ** -0.5),
        'w_ffn_up': nrm(ks[22], (L, D_MODEL, D_FF), D_MODEL ** -0.5),
        'w_ffn_down': nrm(ks[23], (L, D_FF, D_MODEL), D_FF ** -0.5),
        'ln_final': gain(ks[24], (D_MODEL,)),
    }


def reference(x_prompt, x_sample, cache_sb_k, cache_sb_v, cache_mem_k, cache_mem_v, mem_prompt,
              ln_mix, w_in, g_sgu_v, w_sgu, b_sgu, g_out_sgu, g_out_sb, w_out,
              ln_mem, w_mq, w_mk, w_mv, w_mo, ln_ffn, w_ffn_gate, w_ffn_up, w_ffn_down, ln_final):
    x = x_prompt
    B, T = x.shape[0], x.shape[1]
    sbk_p, sbv_p, mk_p, mv_p = [], [], [], []
    for l in range(DEPTH):
        h = _rmsnorm(x, ln_mix[l])
        u, va, q, k, vb = _mixer_inputs(h, w_in[l], g_sgu_v[l])
        va_c = va.reshape(B, T // SGU_CHUNK, SGU_CHUNK, SGU_GROUPS, SGU_GROUP_DIM)
        a_out = u * _sgu_mix(va_c, w_sgu[l], b_sgu[l]).reshape(B, T, SGU_WIDTH)
        sb_out = _sb_prompt(q, k, vb)
        x = x + _merge(a_out, sb_out, g_out_sgu[l], g_out_sb[l], w_out[l])
        mk, mv = _mem_kv(mem_prompt, w_mk[l], w_mv[l])
        x = x + _mem_attend(_rmsnorm(x, ln_mem[l]), mk, mv, w_mq[l], w_mo[l])
        x = x + _swiglu(_rmsnorm(x, ln_ffn[l]), w_ffn_gate[l], w_ffn_up[l], w_ffn_down[l])
        sbk_p.append(k)
        sbv_p.append(vb)
        mk_p.append(mk)
        mv_p.append(mv)
    y_prompt = _rmsnorm(x, ln_final)

    x = x_sample
    Bs, n = x.shape[0], x.shape[1]
    P = cache_sb_k.shape[2]
    k_pos = jnp.arange(P + n, dtype=jnp.int32)
    q_pos = P + jnp.arange(n, dtype=jnp.int32)
    sbk_s, sbv_s, sguv_s = [], [], []
    for l in range(DEPTH):
        h = _rmsnorm(x, ln_mix[l])
        u, va, q, k, vb = _mixer_inputs(h, w_in[l], g_sgu_v[l])
        a_out = u * _sgu_mix(va[:, None], w_sgu[l], b_sgu[l]).reshape(Bs, n, SGU_WIDTH)
        k_all = jnp.concatenate([cache_sb_k[l], k], axis=1)
        v_all = jnp.concatenate([cache_sb_v[l], vb], axis=1)
        sb_out = _stick_breaking(q, k_all, v_all, q_pos, k_pos)
        x = x + _merge(a_out, sb_out, g_out_sgu[l], g_out_sb[l], w_out[l])
        x = x + _mem_attend(_rmsnorm(x, ln_mem[l]), cache_mem_k[l], cache_mem_v[l], w_mq[l], w_mo[l])
        x = x + _swiglu(_rmsnorm(x, ln_ffn[l]), w_ffn_gate[l], w_ffn_up[l], w_ffn_down[l])
        sbk_s.append(k)
        sbv_s.append(vb)
        sguv_s.append(va)
    y_sample = _rmsnorm(x, ln_final)

    sb_k_prompt = jnp.stack(sbk_p)
    sb_v_prompt = jnp.stack(sbv_p)
    mem_k_prompt = jnp.stack(mk_p)
    mem_v_prompt = jnp.stack(mv_p)
    sb_k_sample = jnp.stack(sbk_s)
    sb_v_sample = jnp.stack(sbv_s)
    sgu_v_sample = jnp.stack(sguv_s)
    return (y_prompt, y_sample, sb_k_prompt, sb_v_prompt, mem_k_prompt, mem_v_prompt, sb_k_sample, sb_v_sample, sgu_v_sample)
```

```python
import functools

import jax
import jax.numpy as jnp
from jax import lax
from jax.experimental import pallas as pl
from jax.experimental.pallas import tpu as pltpu

F32 = jnp.float32
BF16 = jnp.bfloat16
EPS = 1e-6

LANES = 128
SGU_CHUNK = 128
SGU_GROUPS = 4
SB_HEADS = 8
MEM_HEADS = 4
ROW_TILE = 512
SB_Q_TILE = 128
SB_K_TILE = 256
SB_HEAD_GROUP = 2
VMEM_LIMIT = 56 * 1024 * 1024


def _params(n_axes):
    return pltpu.CompilerParams(dimension_semantics=("parallel",) * n_axes,
                                vmem_limit_bytes=VMEM_LIMIT)


def _resident(block_shape, index_map):
    return pl.BlockSpec(block_shape, index_map, pipeline_mode=pl.Buffered(1))


def _rmsnorm(x, g):
    return x * lax.rsqrt(jnp.mean(x * x, axis=-1, keepdims=True) + EPS) * g


def _log_keep(z):
    return -(jnp.maximum(z, 0.0) + jnp.log(1.0 + jnp.exp(-jnp.abs(z))))


def _cumsum_matrix():
    r = lax.broadcasted_iota(jnp.int32, (2 * LANES, 2 * LANES), 0) % LANES
    c = lax.broadcasted_iota(jnp.int32, (2 * LANES, 2 * LANES), 1)
    return jnp.where((c >= LANES) | (r >= c), 1.0, 0.0).astype(BF16)


def _block_suffix_sums(lk_blk, uo, carry):
    hi = lk_blk.astype(BF16)
    lo = (lk_blk - hi.astype(F32)).astype(BF16)
    r = jnp.dot(jnp.concatenate([hi, lo], axis=1), uo, preferred_element_type=F32)
    return r[:, :LANES] + carry, carry + r[:, LANES:]


def _in_proj_body(x_ref, ln_ref, w_ref, gv_ref, u_ref, va_ref, qs_ref, k_ref, v_ref, kb_ref, vb_ref,
                  *, width, q_scale):
    h = _rmsnorm(x_ref[...], ln_ref[...]).astype(BF16)

    def proj(j):
        return jnp.dot(h, w_ref[:, j * width:(j + 1) * width], preferred_element_type=F32)

    u_ref[...] = jax.nn.gelu(proj(0))
    va = jax.nn.gelu(proj(1))
    e = width // SGU_GROUPS
    for g in range(SGU_GROUPS):
        va_ref[:, g * e:(g + 1) * e] = _rmsnorm(va[:, g * e:(g + 1) * e], gv_ref[g:g + 1, :])
    qs_ref[...] = (proj(2) * q_scale).astype(BF16)
    k = proj(3)
    k_ref[...] = k
    kb_ref[...] = k.astype(BF16)
    v = proj(4)
    v_ref[...] = v
    vb_ref[...] = v.astype(BF16)


def _in_proj(x, ln, w_in, g_v, layer, tm):
    n, d = x.shape
    width = d // 2
    in_w = w_in.shape[-1]
    row = lambda i: (i, 0)
    lay3 = lambda i: (layer, 0, 0)
    f32_out = jax.ShapeDtypeStruct((n, width), F32)
    bf_out = jax.ShapeDtypeStruct((n, width), BF16)
    out_spec = pl.BlockSpec((tm, width), row)
    return pl.pallas_call(
        functools.partial(_in_proj_body, width=width, q_scale=(width // SB_HEADS) ** -0.5),
        grid=(n // tm,),
        in_specs=[pl.BlockSpec((tm, d), row),
                  _resident((None, 1, d), lay3),
                  _resident((None, d, in_w), lay3),
                  _resident((None, SGU_GROUPS, width // SGU_GROUPS), lay3)],
        out_specs=[out_spec] * 7,
        out_shape=[f32_out, f32_out, bf_out, f32_out, f32_out, bf_out, bf_out],
        compiler_params=_params(1),
        name="in_proj",
    )(x, ln, w_in, g_v)


def _sb_prompt_body(q_ref, kt_ref, v_ref, o_ref, uo_ref, *, heads, tq, kc):
    t_len = q_ref.shape[1]
    dh = q_ref.shape[2]
    uo_ref[...] = _cumsum_matrix()
    n_sub = kc // LANES

    def chunk(c, qs, qpos, state, masked):
        new_state = []
        for g in range(heads):
            carry, acc = state[g]
            z = jnp.dot(qs[g], kt_ref[g, c], preferred_element_type=F32)
            lk = _log_keep(z)
            if masked:
                kpos = c * kc + lax.broadcasted_iota(jnp.int32, (tq, kc), 1)
                valid = kpos < qpos
                lk = jnp.where(valid, lk, 0.0)
            cums = [None] * n_sub
            for s in reversed(range(n_sub)):
                cums[s], carry = _block_suffix_sums(lk[:, s * LANES:(s + 1) * LANES], uo_ref[...], carry)
            a = jnp.exp(z + jnp.concatenate(cums, axis=1))
            if masked:
                a = jnp.where(valid, a, 0.0)
            acc = acc + jnp.dot(a.astype(BF16), v_ref[g, c], preferred_element_type=F32)
            new_state.append((carry, acc))
        return tuple(new_state)

    def q_block(i, _):
        r0 = pl.multiple_of(i * tq, tq)
        c_top = lax.div(i * tq, jnp.int32(kc))
        qpos = r0 + lax.broadcasted_iota(jnp.int32, (tq, kc), 0)
        qs = [q_ref[g, pl.ds(r0, tq), :] for g in range(heads)]
        state = tuple((jnp.zeros((tq, LANES), F32), jnp.zeros((tq, dh), F32)) for _ in range(heads))
        state = chunk(c_top, qs, qpos, state, True)
        state = lax.fori_loop(0, c_top, lambda t, s: chunk(c_top - 1 - t, qs, qpos, s, False), state)
        for g in range(heads):
            o_ref[g, pl.ds(r0, tq), :] = state[g][1]
        return 0

    lax.fori_loop(0, t_len // tq, q_block, 0)


def _sb_prompt(qs, kb, vb, batch, t_len):
    n, width = qs.shape
    dh = width // SB_HEADS
    kc, tq, hg = SB_K_TILE, SB_Q_TILE, SB_HEAD_GROUP
    nc = t_len // kc
    q_h = qs.reshape(batch, t_len, SB_HEADS, dh).transpose(0, 2, 1, 3)
    kt_h = kb.reshape(batch, nc, kc, SB_HEADS, dh).transpose(0, 3, 1, 4, 2)
    v_h = vb.reshape(batch, nc, kc, SB_HEADS, dh).transpose(0, 3, 1, 2, 4)
    o_h = pl.pallas_call(
        functools.partial(_sb_prompt_body, heads=hg, tq=tq, kc=kc),
        grid=(batch, SB_HEADS // hg),
        in_specs=[pl.BlockSpec((None, hg, t_len, dh), lambda b, h: (b, h, 0, 0)),
                  pl.BlockSpec((None, hg, nc, dh, kc), lambda b, h: (b, h, 0, 0, 0)),
                  pl.BlockSpec((None, hg, nc, kc, dh), lambda b, h: (b, h, 0, 0, 0))],
        out_specs=pl.BlockSpec((None, hg, t_len, dh), lambda b, h: (b, h, 0, 0)),
        out_shape=jax.ShapeDtypeStruct((batch, SB_HEADS, t_len, dh), F32),
        scratch_shapes=[pltpu.VMEM((2 * LANES, 2 * LANES), BF16)],
        compiler_params=_params(2),
        name="sb_prompt",
    )(q_h, kt_h, v_h)
    return o_h.transpose(0, 2, 1, 3).reshape(n, width)


def _sb_sample_body(q_ref, kc_ref, vc_ref, kn_ref, vn_ref, o_ref, k_all, v_all, *, past, n_new):
    width = q_ref.shape[1]
    dh = width // SB_HEADS
    rows = SB_HEADS * n_new
    n_keys = k_all.shape[0]
    pad = n_keys - past - n_new
    k_all[0:past, :] = kc_ref[...].astype(BF16)
    v_all[0:past, :] = vc_ref[...].astype(BF16)
    k_all[past:past + n_new, :] = kn_ref[...]
    v_all[past:past + n_new, :] = vn_ref[...]
    k_all[past + n_new:, :] = jnp.zeros((pad, width), BF16)
    v_all[past + n_new:, :] = jnp.zeros((pad, width), BF16)

    q_rep = jnp.concatenate([q_ref[...].astype(F32)] * SB_HEADS, axis=0)
    row_head = lax.broadcasted_iota(jnp.int32, (rows, width), 0) // n_new
    col_head = lax.broadcasted_iota(jnp.int32, (rows, width), 1) // dh
    own = row_head == col_head
    q_bd = jnp.where(own, q_rep, 0.0).astype(BF16)

    z = lax.dot_general(q_bd, k_all[...], (((1,), (1,)), ((), ())), preferred_element_type=F32)
    kpos = lax.broadcasted_iota(jnp.int32, (rows, n_keys), 1)
    qpos = past + lax.broadcasted_iota(jnp.int32, (rows, n_keys), 0) % n_new
    valid = kpos < qpos
    lk = jnp.where(valid, _log_keep(z), 0.0)
    uo = _cumsum_matrix()
    n_blk = n_keys // LANES
    carry = jnp.zeros((rows, LANES), F32)
    cums = [None] * n_blk
    for s in reversed(range(n_blk)):
        cums[s], carry = _block_suffix_sums(lk[:, s * LANES:(s + 1) * LANES], uo, carry)
    a = jnp.where(valid, jnp.exp(z + jnp.concatenate(cums, axis=1)), 0.0)
    o_all = jnp.dot(a.astype(BF16), v_all[...], preferred_element_type=F32)
    o_all = jnp.where(own, o_all, 0.0)
    out = o_all[0:n_new, :]
    for h in range(1, SB_HEADS):
        out = out + o_all[h * n_new:(h + 1) * n_new, :]
    o_ref[...] = out


def _sb_sample(qs, kb, vb, cache_k, cache_v, layer, n_new):
    n, width = qs.shape
    bs = n // n_new
    past = cache_k.shape[2]
    assert past % LANES == 0 and n_new % 8 == 0 and n_new <= LANES
    n_keys = past + LANES
    row = lambda b: (b, 0)
    cache = lambda b: (layer, b, 0, 0)
    return pl.pallas_call(
        functools.partial(_sb_sample_body, past=past, n_new=n_new),
        grid=(bs,),
        in_specs=[pl.BlockSpec((n_new, width), row),
                  pl.BlockSpec((None, None, past, width), cache),
                  pl.BlockSpec((None, None, past, width), cache),
                  pl.BlockSpec((n_new, width), row),
                  pl.BlockSpec((n_new, width), row)],
        out_specs=pl.BlockSpec((n_new, width), row),
        out_shape=jax.ShapeDtypeStruct((n, width), F32),
        scratch_shapes=[pltpu.VMEM((n_keys, width), BF16), pltpu.VMEM((n_keys, width), BF16)],
        compiler_params=_params(1),
        name="sb_sample",
    )(qs, cache_k, cache_v, kb, vb)


def _merge_body(x_ref, u_ref, va_ref, sb_ref, ws_ref, bs_ref, ga_ref, gb_ref, wo_ref, o_ref, cat_ref):
    tm = x_ref.shape[0]
    width = u_ref.shape[1]
    e = width // SGU_GROUPS
    r = lax.broadcasted_iota(jnp.int32, (SGU_CHUNK, SGU_CHUNK), 0)
    c = lax.broadcasted_iota(jnp.int32, (SGU_CHUNK, SGU_CHUNK), 1)
    w_tril = [jnp.where(r >= c, ws_ref[g], 0.0).astype(BF16) for g in range(SGU_GROUPS)]
    for ch in range(tm // SGU_CHUNK):
        rows = slice(ch * SGU_CHUNK, (ch + 1) * SGU_CHUNK)
        va = va_ref[rows, :].astype(BF16)
        mix = jnp.concatenate(
            [jnp.dot(w_tril[g], va[:, g * e:(g + 1) * e], preferred_element_type=F32) + bs_ref[:, g:g + 1]
             for g in range(SGU_GROUPS)], axis=1)
        a_out = u_ref[rows, :] * mix
        cat_ref[rows, 0:width] = _rmsnorm(a_out, ga_ref[...]).astype(BF16)
        cat_ref[rows, width:] = _rmsnorm(sb_ref[rows, :], gb_ref[...]).astype(BF16)
    o_ref[...] = x_ref[...] + jnp.dot(cat_ref[...], wo_ref[...], preferred_element_type=F32)


def _merge(x, u, va, sb, w_mix, b_mix, g_a, g_b, w_out, layer, tm):
    n, d = x.shape
    width = u.shape[1]
    sb_width = sb.shape[1]
    row = lambda i: (i, 0)
    fix2 = lambda i: (0, 0)
    fix3 = lambda i: (0, 0, 0)
    lay3 = lambda i: (layer, 0, 0)
    return pl.pallas_call(
        _merge_body,
        grid=(n // tm,),
        in_specs=[pl.BlockSpec((tm, d), row),
                  pl.BlockSpec((tm, width), row),
                  pl.BlockSpec((tm, width), row),
                  pl.BlockSpec((tm, sb_width), row),
                  _resident((SGU_GROUPS, SGU_CHUNK, SGU_CHUNK), fix3),
                  _resident((SGU_CHUNK, SGU_GROUPS), fix2),
                  _resident((None, 1, width), lay3),
                  _resident((None, 1, sb_width), lay3),
                  _resident((None, width + sb_width, d), lay3)],
        out_specs=pl.BlockSpec((tm, d), row),
        out_shape=jax.ShapeDtypeStruct((n, d), F32),
        scratch_shapes=[pltpu.VMEM((tm, width + sb_width), BF16)],
        compiler_params=_params(1),
        name="merge",
    )(x, u, va, sb, w_mix, b_mix, g_a, g_b, w_out)


def _mem_kv_body(m_ref, wk_ref, wv_ref, k_ref, v_ref):
    m = m_ref[...].astype(BF16)
    k_ref[...] = jnp.dot(m, wk_ref[...], preferred_element_type=F32)
    v_ref[...] = jnp.dot(m, wv_ref[...], preferred_element_type=F32)


def _mem_kv(mem, w_mk, w_mv):
    b, m, d = mem.shape
    depth = w_mk.shape[0]
    out = jax.ShapeDtypeStruct((depth, b, m, d), F32)
    w_spec = pl.BlockSpec((None, d, d), lambda l, i: (l, 0, 0))
    o_spec = pl.BlockSpec((None, None, m, d), lambda l, i: (l, i, 0, 0))
    return pl.pallas_call(
        _mem_kv_body,
        grid=(depth, b),
        in_specs=[pl.BlockSpec((None, m, d), lambda l, i: (i, 0, 0)), w_spec, w_spec],
        out_specs=[o_spec, o_spec],
        out_shape=[out, out],
        compiler_params=_params(2),
        name="mem_kv",
    )(mem, w_mk, w_mv)


def _mem_attn_body(x_ref, ln_ref, wq_ref, mk_ref, mv_ref, wo_ref, o_ref, oc_ref, *, n_streams):
    tm, d = x_ref.shape
    dh = d // MEM_HEADS
    rpb = tm // n_streams
    x = x_ref[...]
    h = _rmsnorm(x, ln_ref[...]).astype(BF16)
    q = (jnp.dot(h, wq_ref[...], preferred_element_type=F32) * dh ** -0.5).astype(BF16)
    for s in range(n_streams):
        rows = slice(s * rpb, (s + 1) * rpb)
        for hd in range(MEM_HEADS):
            cols = slice(hd * dh, (hd + 1) * dh)
            k = mk_ref[s, :, cols].astype(BF16)
            v = mv_ref[s, :, cols].astype(BF16)
            sc = lax.dot_general(q[rows, cols], k, (((1,), (1,)), ((), ())), preferred_element_type=F32)
            p = jnp.exp(sc - jnp.max(sc, axis=-1, keepdims=True))
            o = jnp.dot(p.astype(BF16), v, preferred_element_type=F32) / jnp.sum(p, axis=-1, keepdims=True)
            oc_ref[rows, cols] = o.astype(BF16)
    o_ref[...] = x + jnp.dot(oc_ref[...], wo_ref[...], preferred_element_type=F32)


def _mem_attn(x, ln, w_mq, mk, mv, w_mo, layer, tm, rows_per_stream):
    n, d = x.shape
    m = mk.shape[2]
    n_streams = max(1, tm // rows_per_stream)
    tiles_per_stream = max(1, rows_per_stream // tm)
    row = lambda i: (i, 0)
    lay3 = lambda i: (layer, 0, 0)
    mem = lambda i: (layer, i // tiles_per_stream, 0, 0)
    return pl.pallas_call(
        functools.partial(_mem_attn_body, n_streams=n_streams),
        grid=(n // tm,),
        in_specs=[pl.BlockSpec((tm, d), row),
                  _resident((None, 1, d), lay3),
                  _resident((None, d, d), lay3),
                  pl.BlockSpec((None, n_streams, m, d), mem),
                  pl.BlockSpec((None, n_streams, m, d), mem),
                  _resident((None, d, d), lay3)],
        out_specs=pl.BlockSpec((tm, d), row),
        out_shape=jax.ShapeDtypeStruct((n, d), F32),
        scratch_shapes=[pltpu.VMEM((tm, d), BF16)],
        compiler_params=_params(1),
        name="mem_attn",
    )(x, ln, w_mq, mk, mv, w_mo)


def _ffn_body(x_ref, ln_ref, wg_ref, wu_ref, wd_ref, lnf_ref, o_ref, *, final_norm):
    x = x_ref[...]
    h = _rmsnorm(x, ln_ref[...]).astype(BF16)
    gate = jnp.dot(h, wg_ref[...], preferred_element_type=F32)
    up = jnp.dot(h, wu_ref[...], preferred_element_type=F32)
    act = (jax.nn.silu(gate) * up).astype(BF16)
    y = x + jnp.dot(act, wd_ref[...], preferred_element_type=F32)
    if final_norm:
        y = _rmsnorm(y, lnf_ref[...])
    o_ref[...] = y


def _ffn(x, ln, w_gate, w_up, w_down, ln_final, layer, tm, final_norm):
    n, d = x.shape
    d_ff = w_gate.shape[-1]
    row = lambda i: (i, 0)
    lay3 = lambda i: (layer, 0, 0)
    return pl.pallas_call(
        functools.partial(_ffn_body, final_norm=final_norm),
        grid=(n // tm,),
        in_specs=[pl.BlockSpec((tm, d), row),
                  _resident((None, 1, d), lay3),
                  _resident((None, d, d_ff), lay3),
                  _resident((None, d, d_ff), lay3),
                  _resident((None, d_ff, d), lay3),
                  _resident((1, d), lambda i: (0, 0))],
        out_specs=pl.BlockSpec((tm, d), row),
        out_shape=jax.ShapeDtypeStruct((n, d), F32),
        compiler_params=_params(1),
        name="ffn",
    )(x, ln, w_gate, w_up, w_down, ln_final)


def kernel(x_prompt, x_sample, cache_sb_k, cache_sb_v, cache_mem_k, cache_mem_v, mem_prompt, ln_mix, w_in, g_sgu_v, w_sgu, b_sgu, g_out_sgu, g_out_sb, w_out, ln_mem, w_mq, w_mk, w_mv, w_mo, ln_ffn, w_ffn_gate, w_ffn_up, w_ffn_down, ln_final):
    batch, t_len, d = x_prompt.shape
    bs, n_new, _ = x_sample.shape
    depth = w_in.shape[0]
    past = cache_sb_k.shape[2]
    n_mem = mem_prompt.shape[1]
    width = d // 2
    dh = width // SB_HEADS
    e = width // SGU_GROUPS
    assert t_len % ROW_TILE == 0 and t_len % SB_K_TILE == 0 and SGU_CHUNK % n_new == 0

    w_in_b, w_out_b = w_in.astype(BF16), w_out.astype(BF16)
    w_mq_b, w_mk_b, w_mv_b, w_mo_b = (w.astype(BF16) for w in (w_mq, w_mk, w_mv, w_mo))
    w_g_b, w_u_b, w_d_b = (w.astype(BF16) for w in (w_ffn_gate, w_ffn_up, w_ffn_down))
    as_rows = lambda g: g.reshape(depth, 1, -1)
    ln_mix_r, ln_mem_r, ln_ffn_r = as_rows(ln_mix), as_rows(ln_mem), as_rows(ln_ffn)
    g_a_r, g_b_r = as_rows(g_out_sgu), as_rows(g_out_sb)
    ln_final_r = ln_final.reshape(1, d)

    cache_k = cache_sb_k.reshape(depth, bs, past, width)
    cache_v = cache_sb_v.reshape(depth, bs, past, width)
    cmem_k = cache_mem_k.reshape(depth, bs, n_mem, d)
    cmem_v = cache_mem_v.reshape(depth, bs, n_mem, d)

    mk_p, mv_p = _mem_kv(mem_prompt, w_mk_b, w_mv_b)

    n_p = batch * t_len
    x = x_prompt.reshape(n_p, d)
    sbk_p, sbv_p = [], []
    for l in range(depth):
        u, va, qs, k, v, kb, vb = _in_proj(x, ln_mix_r, w_in_b, g_sgu_v, l, ROW_TILE)
        sb = _sb_prompt(qs, kb, vb, batch, t_len)
        x = _merge(x, u, va, sb, w_sgu[l], b_sgu[l].T, g_a_r, g_b_r, w_out_b, l, ROW_TILE)
        x = _mem_attn(x, ln_mem_r, w_mq_b, mk_p, mv_p, w_mo_b, l, ROW_TILE, t_len)
        x = _ffn(x, ln_ffn_r, w_g_b, w_u_b, w_d_b, ln_final_r, l, ROW_TILE, l == depth - 1)
        sbk_p.append(k)
        sbv_p.append(v)
    y_prompt = x.reshape(batch, t_len, d)

    n_s = bs * n_new
    tm_s = min(ROW_TILE, n_s)
    tm_mem = 4 * n_new
    reps = SGU_CHUNK // n_new
    x = x_sample.reshape(n_s, d)
    sbk_s, sbv_s, sguv_s = [], [], []
    for l in range(depth):
        u, va, qs, k, v, kb, vb = _in_proj(x, ln_mix_r, w_in_b, g_sgu_v, l, tm_s)
        sb = _sb_sample(qs, kb, vb, cache_k, cache_v, l, n_new)
        w_blk = jnp.einsum('ab,gts->gatbs', jnp.eye(reps, dtype=F32), w_sgu[l][:, :n_new, :n_new])
        w_blk = w_blk.reshape(SGU_GROUPS, SGU_CHUNK, SGU_CHUNK)
        b_blk = jnp.tile(b_sgu[l][:, :n_new], (1, reps)).T
        x = _merge(x, u, va, sb, w_blk, b_blk, g_a_r, g_b_r, w_out_b, l, tm_s)
        x = _mem_attn(x, ln_mem_r, w_mq_b, cmem_k, cmem_v, w_mo_b, l, tm_mem, n_new)
        x = _ffn(x, ln_ffn_r, w_g_b, w_u_b, w_d_b, ln_final_r, l, tm_s, l == depth - 1)
        sbk_s.append(k)
        sbv_s.append(v)
        sguv_s.append(va)
    y_sample = x.reshape(bs, n_new, d)

    heads = lambda a, b_, t_: a.reshape(b_, t_, SB_HEADS, dh)
    return (y_prompt, y_sample,
            jnp.stack([heads(a, batch, t_len) for a in sbk_p]),
            jnp.stack([heads(a, batch, t_len) for a in sbv_p]),
            mk_p.reshape(depth, batch, n_mem, MEM_HEADS, d // MEM_HEADS),
            mv_p.reshape(depth, batch, n_mem, MEM_HEADS, d // MEM_HEADS),
            jnp.stack([heads(a, bs, n_new) for a in sbk_s]),
            jnp.stack([heads(a, bs, n_new) for a in sbv_s]),
            jnp.stack([a.reshape(bs, n_new, SGU_GROUPS, e) for a in sguv_s]))
```

```python
import functools

import jax
import jax.numpy as jnp
from jax import lax
from jax.experimental import pallas as pl
from jax.experimental.pallas import tpu as pltpu

F32 = jnp.float32
BF16 = jnp.bfloat16
EPS = 1e-6

LANES = 128
SGU_CHUNK = 128
SGU_GROUPS = 4
SB_HEADS = 8
MEM_HEADS = 4
ROW_TILE = 512
SB_TILE = 256
SB_GROUP = 4
SB_SPLIT = 2
LOG2E = 1.4426950408889634
VMEM_LIMIT = 56 * 1024 * 1024


def _params(n_axes):
    return pltpu.CompilerParams(dimension_semantics=("parallel",) * n_axes,
                                vmem_limit_bytes=VMEM_LIMIT)


def _resident(block_shape, index_map):
    return pl.BlockSpec(block_shape, index_map, pipeline_mode=pl.Buffered(1))


def _rmsnorm(x, g):
    return x * lax.rsqrt(jnp.mean(x * x, axis=-1, keepdims=True) + EPS) * g


def _log_keep(z):
    return -(jnp.maximum(z, 0.0) + jnp.log(1.0 + jnp.exp(-jnp.abs(z))))


def _cumsum_matrix():
    r = lax.broadcasted_iota(jnp.int32, (2 * LANES, 2 * LANES), 0) % LANES
    c = lax.broadcasted_iota(jnp.int32, (2 * LANES, 2 * LANES), 1)
    return jnp.where((c >= LANES) | (r >= c), 1.0, 0.0).astype(BF16)


def _block_suffix_sums(lk_blk, uo, carry):
    hi = lk_blk.astype(BF16)
    lo = (lk_blk - hi.astype(F32)).astype(BF16)
    r = jnp.dot(jnp.concatenate([hi, lo], axis=1), uo, preferred_element_type=F32)
    return r[:, :LANES] + carry, carry + r[:, LANES:]


def _in_proj_body(x_ref, ln_ref, w_ref, gv_ref, *refs, width, q_scale, n_stacks):
    u_ref, va_ref, qs_ref, kb_ref, vb_ref, k_ref, v_ref = refs[n_stacks:n_stacks + 7]
    h = _rmsnorm(x_ref[...], ln_ref[...]).astype(BF16)

    def proj(j):
        return jnp.dot(h, w_ref[:, j * width:(j + 1) * width], preferred_element_type=F32)

    u_ref[...] = jax.nn.gelu(proj(0))
    va = jax.nn.gelu(proj(1))
    e = width // SGU_GROUPS
    for g in range(SGU_GROUPS):
        va_ref[:, g * e:(g + 1) * e] = _rmsnorm(va[:, g * e:(g + 1) * e], gv_ref[g:g + 1, :])
    qs_ref[...] = (proj(2) * q_scale).astype(BF16)
    k = proj(3)
    k_ref[...] = k
    kb_ref[...] = k.astype(BF16)
    v = proj(4)
    v_ref[...] = v
    vb_ref[...] = v.astype(BF16)


def _in_proj(x, ln, w_in, g_v, k_stack, v_stack, va_stack, layer, tm):
    n, d = x.shape
    width = d // 2
    in_w = w_in.shape[-1]
    row = lambda i: (i, 0)
    lay3 = lambda i: (layer, 0, 0)
    stacks = [k_stack, v_stack] + ([va_stack] if va_stack is not None else [])
    f32_out = jax.ShapeDtypeStruct((n, width), F32)
    bf_out = jax.ShapeDtypeStruct((n, width), BF16)
    out_spec = pl.BlockSpec((tm, width), row)
    stack_spec = pl.BlockSpec((None, tm, width), lambda i: (layer, i, 0))
    va_shape, va_spec = ((va_stack, stack_spec) if va_stack is not None else
                         (jax.ShapeDtypeStruct((1, n, width), F32), pl.BlockSpec((None, tm, width), lambda i: (0, i, 0))))
    n_in = 4
    outs = pl.pallas_call(
        functools.partial(_in_proj_body, width=width, q_scale=(width // SB_HEADS) ** -0.5, n_stacks=len(stacks)),
        grid=(n // tm,),
        in_specs=[pl.BlockSpec((tm, d), row),
                  _resident((None, 1, d), lay3),
                  _resident((None, d, in_w), lay3),
                  _resident((None, SGU_GROUPS, width // SGU_GROUPS), lay3)]
                 + [pl.BlockSpec(memory_space=pl.ANY)] * len(stacks),
        out_specs=[out_spec, va_spec, out_spec, out_spec, out_spec, stack_spec, stack_spec],
        out_shape=[f32_out, jax.ShapeDtypeStruct(va_shape.shape, F32), bf_out, bf_out, bf_out,
                   jax.ShapeDtypeStruct(k_stack.shape, F32), jax.ShapeDtypeStruct(v_stack.shape, F32)],
        input_output_aliases={n_in: 5, n_in + 1: 6, **({n_in + 2: 1} if va_stack is not None else {})},
        compiler_params=_params(1),
        name="in_proj",
    )(x, ln, w_in, g_v, *stacks)
    return outs


def _neg_log2_keep(z2):
    neg_abs = lax.bitcast_convert_type(lax.bitcast_convert_type(z2, jnp.int32) | jnp.int32(-2 ** 31), F32)
    return jnp.maximum(z2, 0.0) + jnp.log2(1.0 + jnp.exp2(neg_abs))


def _neg_cumsum_matrix():
    r = lax.broadcasted_iota(jnp.int32, (2 * LANES, 2 * LANES), 0) % LANES
    c = lax.broadcasted_iota(jnp.int32, (2 * LANES, 2 * LANES), 1)
    return jnp.where((c >= LANES) | (r >= c), -1.0, 0.0).astype(BF16)


def _sb_prompt_body(q_ref, k_ref, v_ref, o_ref, nuo_ref, qm_ref, z2_ref, l_ref, a_ref, carry_ref, acc_ref, *, dh):
    t_len, gw = q_ref.shape
    tile = SB_TILE
    n_heads = gw // dh
    rows = n_heads * tile
    half_rows = rows // SB_SPLIT
    slabs = half_rows // LANES
    nuo_ref[...] = _neg_cumsum_matrix()
    lane_head = lax.broadcasted_iota(jnp.int32, (tile, gw), 1) // dh
    nt_dims = (((1,), (1,)), ((), ()))

    def fold(c, masked):
        k0 = pl.multiple_of(c * tile, tile)
        k_c = k_ref[pl.ds(k0, tile), :]
        v_c = v_ref[pl.ds(k0, tile), :]
        for half in range(SB_SPLIT):
            base = half * half_rows
            z_all = lax.dot_general(qm_ref[base:base + half_rows, :], k_c, nt_dims, preferred_element_type=F32)
            for sl in range(slabs):
                r0 = base + sl * LANES
                z2 = z_all[sl * LANES:(sl + 1) * LANES] * LOG2E
                nlk = _neg_log2_keep(z2)
                if masked:
                    qpos = (r0 % tile) + lax.broadcasted_iota(jnp.int32, (LANES, tile), 0)
                    kpos = lax.broadcasted_iota(jnp.int32, (LANES, tile), 1)
                    nlk = jnp.where(kpos < qpos, nlk, 0.0)
                z2_ref[r0:r0 + LANES, :] = z2
                hi = nlk.astype(BF16)
                lo = (nlk - hi.astype(F32)).astype(BF16)
                for s in range(tile // LANES):
                    blk = slice(s * LANES, (s + 1) * LANES)
                    l_ref[s, r0:r0 + LANES, :] = jnp.concatenate([hi[:, blk], lo[:, blk]], axis=1)
            l_all = jnp.concatenate([l_ref[1, base:base + half_rows, :], l_ref[0, base:base + half_rows, :]], axis=0)
            r_all = jnp.dot(l_all, nuo_ref[...], preferred_element_type=F32)
            for sl in range(slabs):
                r0 = base + sl * LANES
                r1 = r_all[sl * LANES:(sl + 1) * LANES]
                rz = r_all[half_rows + sl * LANES:half_rows + (sl + 1) * LANES]
                carry = carry_ref[r0:r0 + LANES, :]
                cum1 = r1[:, :LANES] + carry
                carry = carry + r1[:, LANES:]
                cum0 = rz[:, :LANES] + carry
                carry_ref[r0:r0 + LANES, :] = carry + rz[:, LANES:]
                a = jnp.exp2(z2_ref[r0:r0 + LANES, :] + jnp.concatenate([cum0, cum1], axis=1))
                if masked:
                    qpos = (r0 % tile) + lax.broadcasted_iota(jnp.int32, (LANES, tile), 0)
                    kpos = lax.broadcasted_iota(jnp.int32, (LANES, tile), 1)
                    a = jnp.where(kpos < qpos, a, 0.0)
                a_ref[r0:r0 + LANES, :] = a.astype(BF16)
            acc_ref[base:base + half_rows, :] += jnp.dot(a_ref[base:base + half_rows, :], v_c,
                                                         preferred_element_type=F32)

    def q_tile(j, _):
        q0 = pl.multiple_of(j * tile, tile)
        q = q_ref[pl.ds(q0, tile), :].astype(F32)
        for h in range(n_heads):
            qm_ref[h * tile:(h + 1) * tile, :] = jnp.where(lane_head == h, q, 0.0).astype(BF16)
        carry_ref[...] = jnp.zeros_like(carry_ref)
        acc_ref[...] = jnp.zeros_like(acc_ref)
        fold(j, True)

        def earlier(t, carry):
            fold(j - 1 - t, False)
            return carry

        lax.fori_loop(0, j, earlier, 0)
        out = acc_ref[0:tile, :]
        for h in range(1, n_heads):
            out = jnp.where(lane_head == h, acc_ref[h * tile:(h + 1) * tile, :], out)
        o_ref[pl.ds(q0, tile), :] = out
        return 0

    lax.fori_loop(0, t_len // tile, q_tile, 0)


def _sb_prompt(qs, kb, vb, batch, t_len):
    n, width = qs.shape
    dh = width // SB_HEADS
    gw = SB_GROUP * dh
    groups = width // gw
    rows = SB_GROUP * SB_TILE
    blk = pl.BlockSpec((t_len, gw), lambda b, g: (b, g))
    return pl.pallas_call(
        functools.partial(_sb_prompt_body, dh=dh),
        grid=(batch, groups),
        in_specs=[blk, blk, blk],
        out_specs=blk,
        out_shape=jax.ShapeDtypeStruct((n, width), F32),
        scratch_shapes=[pltpu.VMEM((2 * LANES, 2 * LANES), BF16),
                        pltpu.VMEM((rows, gw), BF16),
                        pltpu.VMEM((rows, SB_TILE), F32),
                        pltpu.VMEM((SB_TILE // LANES, rows, 2 * LANES), BF16),
                        pltpu.VMEM((rows, SB_TILE), BF16),
                        pltpu.VMEM((rows, LANES), F32),
                        pltpu.VMEM((rows, gw), F32)],
        compiler_params=_params(2),
        name="sb_prompt",
    )(qs, kb, vb)


def _sb_sample_body(q_ref, kc_ref, vc_ref, kn_ref, vn_ref, o_ref, k_all, v_all, *, past, n_new):
    width = q_ref.shape[1]
    dh = width // SB_HEADS
    rows = SB_HEADS * n_new
    n_keys = k_all.shape[0]
    pad = n_keys - past - n_new
    k_all[0:past, :] = kc_ref[...].astype(BF16)
    v_all[0:past, :] = vc_ref[...].astype(BF16)
    k_all[past:past + n_new, :] = kn_ref[...]
    v_all[past:past + n_new, :] = vn_ref[...]
    k_all[past + n_new:, :] = jnp.zeros((pad, width), BF16)
    v_all[past + n_new:, :] = jnp.zeros((pad, width), BF16)

    q_rep = jnp.concatenate([q_ref[...].astype(F32)] * SB_HEADS, axis=0)
    row_head = lax.broadcasted_iota(jnp.int32, (rows, width), 0) // n_new
    col_head = lax.broadcasted_iota(jnp.int32, (rows, width), 1) // dh
    own = row_head == col_head
    q_bd = jnp.where(own, q_rep, 0.0).astype(BF16)

    z = lax.dot_general(q_bd, k_all[...], (((1,), (1,)), ((), ())), preferred_element_type=F32)
    kpos = lax.broadcasted_iota(jnp.int32, (rows, n_keys), 1)
    qpos = past + lax.broadcasted_iota(jnp.int32, (rows, n_keys), 0) % n_new
    valid = kpos < qpos
    lk = jnp.where(valid, _log_keep(z), 0.0)
    uo = _cumsum_matrix()
    n_blk = n_keys // LANES
    carry = jnp.zeros((rows, LANES), F32)
    cums = [None] * n_blk
    for s in reversed(range(n_blk)):
        cums[s], carry = _block_suffix_sums(lk[:, s * LANES:(s + 1) * LANES], uo, carry)
    a = jnp.where(valid, jnp.exp(z + jnp.concatenate(cums, axis=1)), 0.0)
    o_all = jnp.dot(a.astype(BF16), v_all[...], preferred_element_type=F32)
    o_all = jnp.where(own, o_all, 0.0)
    out = o_all[0:n_new, :]
    for h in range(1, SB_HEADS):
        out = out + o_all[h * n_new:(h + 1) * n_new, :]
    o_ref[...] = out


def _sb_sample(qs, kb, vb, cache_k, cache_v, layer, n_new):
    n, width = qs.shape
    bs = n // n_new
    past = cache_k.shape[2]
    assert past % LANES == 0 and n_new % 8 == 0 and n_new <= LANES
    n_keys = past + LANES
    row = lambda b: (b, 0)
    cache = lambda b: (layer, b, 0, 0)
    return pl.pallas_call(
        functools.partial(_sb_sample_body, past=past, n_new=n_new),
        grid=(bs,),
        in_specs=[pl.BlockSpec((n_new, width), row),
                  pl.BlockSpec((None, None, past, width), cache),
                  pl.BlockSpec((None, None, past, width), cache),
                  pl.BlockSpec((n_new, width), row),
                  pl.BlockSpec((n_new, width), row)],
        out_specs=pl.BlockSpec((n_new, width), row),
        out_shape=jax.ShapeDtypeStruct((n, width), F32),
        scratch_shapes=[pltpu.VMEM((n_keys, width), BF16), pltpu.VMEM((n_keys, width), BF16)],
        compiler_params=_params(1),
        name="sb_sample",
    )(qs, cache_k, cache_v, kb, vb)


def _merge_body(x_ref, u_ref, va_ref, sb_ref, ws_ref, bs_ref, ga_ref, gb_ref, wo_ref, o_ref, cat_ref):
    tm = x_ref.shape[0]
    width = u_ref.shape[1]
    e = width // SGU_GROUPS
    r = lax.broadcasted_iota(jnp.int32, (SGU_CHUNK, SGU_CHUNK), 0)
    c = lax.broadcasted_iota(jnp.int32, (SGU_CHUNK, SGU_CHUNK), 1)
    w_tril = [jnp.where(r >= c, ws_ref[g], 0.0).astype(BF16) for g in range(SGU_GROUPS)]
    for ch in range(tm // SGU_CHUNK):
        rows = slice(ch * SGU_CHUNK, (ch + 1) * SGU_CHUNK)
        va = va_ref[rows, :].astype(BF16)
        mix = jnp.concatenate(
            [jnp.dot(w_tril[g], va[:, g * e:(g + 1) * e], preferred_element_type=F32) + bs_ref[:, g:g + 1]
             for g in range(SGU_GROUPS)], axis=1)
        a_out = u_ref[rows, :] * mix
        cat_ref[rows, 0:width] = _rmsnorm(a_out, ga_ref[...]).astype(BF16)
        cat_ref[rows, width:] = _rmsnorm(sb_ref[rows, :], gb_ref[...]).astype(BF16)
    o_ref[...] = x_ref[...] + jnp.dot(cat_ref[...], wo_ref[...], preferred_element_type=F32)


def _merge(x, u, va, va_layer, sb, w_mix, b_mix, g_a, g_b, w_out, layer, tm):
    n, d = x.shape
    width = u.shape[1]
    sb_width = sb.shape[1]
    row = lambda i: (i, 0)
    fix2 = lambda i: (0, 0)
    fix3 = lambda i: (0, 0, 0)
    lay3 = lambda i: (layer, 0, 0)
    return pl.pallas_call(
        _merge_body,
        grid=(n // tm,),
        in_specs=[pl.BlockSpec((tm, d), row),
                  pl.BlockSpec((tm, width), row),
                  pl.BlockSpec((None, tm, width), lambda i: (va_layer, i, 0)),
                  pl.BlockSpec((tm, sb_width), row),
                  _resident((SGU_GROUPS, SGU_CHUNK, SGU_CHUNK), fix3),
                  _resident((SGU_CHUNK, SGU_GROUPS), fix2),
                  _resident((None, 1, width), lay3),
                  _resident((None, 1, sb_width), lay3),
                  _resident((None, width + sb_width, d), lay3)],
        out_specs=pl.BlockSpec((tm, d), row),
        out_shape=jax.ShapeDtypeStruct((n, d), F32),
        scratch_shapes=[pltpu.VMEM((tm, width + sb_width), BF16)],
        compiler_params=_params(1),
        name="merge",
    )(x, u, va, sb, w_mix, b_mix, g_a, g_b, w_out)


def _mem_kv_body(m_ref, wk_ref, wv_ref, k_ref, v_ref):
    m = m_ref[...].astype(BF16)
    k_ref[...] = jnp.dot(m, wk_ref[...], preferred_element_type=F32)
    v_ref[...] = jnp.dot(m, wv_ref[...], preferred_element_type=F32)


def _mem_kv(mem, w_mk, w_mv):
    b, m, d = mem.shape
    depth = w_mk.shape[0]
    out = jax.ShapeDtypeStruct((depth, b, m, d), F32)
    w_spec = pl.BlockSpec((None, d, d), lambda l, i: (l, 0, 0))
    o_spec = pl.BlockSpec((None, None, m, d), lambda l, i: (l, i, 0, 0))
    return pl.pallas_call(
        _mem_kv_body,
        grid=(depth, b),
        in_specs=[pl.BlockSpec((None, m, d), lambda l, i: (i, 0, 0)), w_spec, w_spec],
        out_specs=[o_spec, o_spec],
        out_shape=[out, out],
        compiler_params=_params(2),
        name="mem_kv",
    )(mem, w_mk, w_mv)


def _mem_attn_body(x_ref, ln_ref, wq_ref, mk_ref, mv_ref, wo_ref, o_ref, oc_ref, *, n_streams):
    tm, d = x_ref.shape
    dh = d // MEM_HEADS
    rpb = tm // n_streams
    x = x_ref[...]
    h = _rmsnorm(x, ln_ref[...]).astype(BF16)
    q = (jnp.dot(h, wq_ref[...], preferred_element_type=F32) * dh ** -0.5).astype(BF16)
    for s in range(n_streams):
        rows = slice(s * rpb, (s + 1) * rpb)
        for hd in range(MEM_HEADS):
            cols = slice(hd * dh, (hd + 1) * dh)
            k = mk_ref[s, :, cols].astype(BF16)
            v = mv_ref[s, :, cols].astype(BF16)
            sc = lax.dot_general(q[rows, cols], k, (((1,), (1,)), ((), ())), preferred_element_type=F32)
            p = jnp.exp(sc - jnp.max(sc, axis=-1, keepdims=True))
            o = jnp.dot(p.astype(BF16), v, preferred_element_type=F32) / jnp.sum(p, axis=-1, keepdims=True)
            oc_ref[rows, cols] = o.astype(BF16)
    o_ref[...] = x + jnp.dot(oc_ref[...], wo_ref[...], preferred_element_type=F32)


def _mem_attn(x, ln, w_mq, mk, mv, w_mo, layer, tm, rows_per_stream):
    n, d = x.shape
    m = mk.shape[2]
    n_streams = max(1, tm // rows_per_stream)
    tiles_per_stream = max(1, rows_per_stream // tm)
    row = lambda i: (i, 0)
    lay3 = lambda i: (layer, 0, 0)
    mem = lambda i: (layer, i // tiles_per_stream, 0, 0)
    return pl.pallas_call(
        functools.partial(_mem_attn_body, n_streams=n_streams),
        grid=(n // tm,),
        in_specs=[pl.BlockSpec((tm, d), row),
                  _resident((None, 1, d), lay3),
                  _resident((None, d, d), lay3),
                  pl.BlockSpec((None, n_streams, m, d), mem),
                  pl.BlockSpec((None, n_streams, m, d), mem),
                  _resident((None, d, d), lay3)],
        out_specs=pl.BlockSpec((tm, d), row),
        out_shape=jax.ShapeDtypeStruct((n, d), F32),
        scratch_shapes=[pltpu.VMEM((tm, d), BF16)],
        compiler_params=_params(1),
        name="mem_attn",
    )(x, ln, w_mq, mk, mv, w_mo)


def _ffn_body(x_ref, ln_ref, wg_ref, wu_ref, wd_ref, lnf_ref, o_ref, *, final_norm):
    x = x_ref[...]
    h = _rmsnorm(x, ln_ref[...]).astype(BF16)
    gate = jnp.dot(h, wg_ref[...], preferred_element_type=F32)
    up = jnp.dot(h, wu_ref[...], preferred_element_type=F32)
    act = (jax.nn.silu(gate) * up).astype(BF16)
    y = x + jnp.dot(act, wd_ref[...], preferred_element_type=F32)
    if final_norm:
        y = _rmsnorm(y, lnf_ref[...])
    o_ref[...] = y


def _ffn(x, ln, w_gate, w_up, w_down, ln_final, layer, tm, final_norm):
    n, d = x.shape
    d_ff = w_gate.shape[-1]
    row = lambda i: (i, 0)
    lay3 = lambda i: (layer, 0, 0)
    return pl.pallas_call(
        functools.partial(_ffn_body, final_norm=final_norm),
        grid=(n // tm,),
        in_specs=[pl.BlockSpec((tm, d), row),
                  _resident((None, 1, d), lay3),
                  _resident((None, d, d_ff), lay3),
                  _resident((None, d, d_ff), lay3),
                  _resident((None, d_ff, d), lay3),
                  _resident((1, d), lambda i: (0, 0))],
        out_specs=pl.BlockSpec((tm, d), row),
        out_shape=jax.ShapeDtypeStruct((n, d), F32),
        compiler_params=_params(1),
        name="ffn",
    )(x, ln, w_gate, w_up, w_down, ln_final)


def kernel(x_prompt, x_sample, cache_sb_k, cache_sb_v, cache_mem_k, cache_mem_v, mem_prompt, ln_mix, w_in, g_sgu_v, w_sgu, b_sgu, g_out_sgu, g_out_sb, w_out, ln_mem, w_mq, w_mk, w_mv, w_mo, ln_ffn, w_ffn_gate, w_ffn_up, w_ffn_down, ln_final):
    batch, t_len, d = x_prompt.shape
    bs, n_new, _ = x_sample.shape
    depth = w_in.shape[0]
    past = cache_sb_k.shape[2]
    n_mem = mem_prompt.shape[1]
    width = d // 2
    dh = width // SB_HEADS
    e = width // SGU_GROUPS
    assert t_len % ROW_TILE == 0 and t_len % SB_TILE == 0 and SGU_CHUNK % n_new == 0

    w_in_b, w_out_b = w_in.astype(BF16), w_out.astype(BF16)
    w_mq_b, w_mk_b, w_mv_b, w_mo_b = (w.astype(BF16) for w in (w_mq, w_mk, w_mv, w_mo))
    w_g_b, w_u_b, w_d_b = (w.astype(BF16) for w in (w_ffn_gate, w_ffn_up, w_ffn_down))
    as_rows = lambda g: g.reshape(depth, 1, -1)
    ln_mix_r, ln_mem_r, ln_ffn_r = as_rows(ln_mix), as_rows(ln_mem), as_rows(ln_ffn)
    g_a_r, g_b_r = as_rows(g_out_sgu), as_rows(g_out_sb)
    ln_final_r = ln_final.reshape(1, d)

    cache_k = cache_sb_k.reshape(depth, bs, past, width)
    cache_v = cache_sb_v.reshape(depth, bs, past, width)
    cmem_k = cache_mem_k.reshape(depth, bs, n_mem, d)
    cmem_v = cache_mem_v.reshape(depth, bs, n_mem, d)

    mk_p, mv_p = _mem_kv(mem_prompt, w_mk_b, w_mv_b)

    n_p = batch * t_len
    x = x_prompt.reshape(n_p, d)
    k_st = jnp.zeros((depth, n_p, width), F32)
    v_st = jnp.zeros((depth, n_p, width), F32)
    for l in range(depth):
        u, va, qs, kb, vb, k_st, v_st = _in_proj(x, ln_mix_r, w_in_b, g_sgu_v, k_st, v_st, None, l, ROW_TILE)
        sb = _sb_prompt(qs, kb, vb, batch, t_len)
        x = _merge(x, u, va, 0, sb, w_sgu[l], b_sgu[l].T, g_a_r, g_b_r, w_out_b, l, ROW_TILE)
        x = _mem_attn(x, ln_mem_r, w_mq_b, mk_p, mv_p, w_mo_b, l, ROW_TILE, t_len)
        x = _ffn(x, ln_ffn_r, w_g_b, w_u_b, w_d_b, ln_final_r, l, ROW_TILE, l == depth - 1)
    y_prompt = x.reshape(batch, t_len, d)
    sb_k_prompt = k_st.reshape(depth, batch, t_len, SB_HEADS, dh)
    sb_v_prompt = v_st.reshape(depth, batch, t_len, SB_HEADS, dh)

    n_s = bs * n_new
    tm_s = min(ROW_TILE, n_s)
    tm_mem = 4 * n_new
    reps = SGU_CHUNK // n_new
    x = x_sample.reshape(n_s, d)
    k_st = jnp.zeros((depth, n_s, width), F32)
    v_st = jnp.zeros((depth, n_s, width), F32)
    va_st = jnp.zeros((depth, n_s, width), F32)
    for l in range(depth):
        u, va_st, qs, kb, vb, k_st, v_st = _in_proj(x, ln_mix_r, w_in_b, g_sgu_v, k_st, v_st, va_st, l, tm_s)
        sb = _sb_sample(qs, kb, vb, cache_k, cache_v, l, n_new)
        w_blk = jnp.einsum('ab,gts->gatbs', jnp.eye(reps, dtype=F32), w_sgu[l][:, :n_new, :n_new])
        w_blk = w_blk.reshape(SGU_GROUPS, SGU_CHUNK, SGU_CHUNK)
        b_blk = jnp.tile(b_sgu[l][:, :n_new], (1, reps)).T
        x = _merge(x, u, va_st, l, sb, w_blk, b_blk, g_a_r, g_b_r, w_out_b, l, tm_s)
        x = _mem_attn(x, ln_mem_r, w_mq_b, cmem_k, cmem_v, w_mo_b, l, tm_mem, n_new)
        x = _ffn(x, ln_ffn_r, w_g_b, w_u_b, w_d_b, ln_final_r, l, tm_s, l == depth - 1)
    y_sample = x.reshape(bs, n_new, d)

    return (y_prompt, y_sample, sb_k_prompt, sb_v_prompt,
            mk_p.reshape(depth, batch, n_mem, MEM_HEADS, d // MEM_HEADS),
            mv_p.reshape(depth, batch, n_mem, MEM_HEADS, d // MEM_HEADS),
            k_st.reshape(depth, bs, n_new, SB_HEADS, dh),
            v_st.reshape(depth, bs, n_new, SB_HEADS, dh),
            va_st.reshape(depth, bs, n_new, SGU_GROUPS, e))
```

```python
import functools

import jax
import jax.numpy as jnp
from jax import lax
from jax.experimental import pallas as pl
from jax.experimental.pallas import tpu as pltpu

F32 = jnp.float32
BF16 = jnp.bfloat16
EPS = 1e-6

LANES = 128
SGU_CHUNK = 128
SGU_GROUPS = 4
SB_HEADS = 8
MEM_HEADS = 4
ROW_TILE = 512
SB_TILE = 256
SB_GROUP = 4
SB_SPLIT = 2
LOG2E = 1.4426950408889634
SB_EXIT_LOG2 = -152.0
VMEM_LIMIT = 56 * 1024 * 1024


def _params(n_axes):
    return pltpu.CompilerParams(dimension_semantics=("parallel",) * n_axes,
                                vmem_limit_bytes=VMEM_LIMIT)


def _resident(block_shape, index_map):
    return pl.BlockSpec(block_shape, index_map, pipeline_mode=pl.Buffered(1))


def _rmsnorm(x, g):
    return x * lax.rsqrt(jnp.mean(x * x, axis=-1, keepdims=True) + EPS) * g


def _log_keep(z):
    return -(jnp.maximum(z, 0.0) + jnp.log(1.0 + jnp.exp(-jnp.abs(z))))


def _cumsum_matrix():
    r = lax.broadcasted_iota(jnp.int32, (2 * LANES, 2 * LANES), 0) % LANES
    c = lax.broadcasted_iota(jnp.int32, (2 * LANES, 2 * LANES), 1)
    return jnp.where((c >= LANES) | (r >= c), 1.0, 0.0).astype(BF16)


def _block_suffix_sums(lk_blk, uo, carry):
    hi = lk_blk.astype(BF16)
    lo = (lk_blk - hi.astype(F32)).astype(BF16)
    r = jnp.dot(jnp.concatenate([hi, lo], axis=1), uo, preferred_element_type=F32)
    return r[:, :LANES] + carry, carry + r[:, LANES:]


def _in_proj_body(x_ref, ln_ref, w_ref, gv_ref, *refs, width, q_scale, n_stacks, kv_transposed):
    n_extra = 1 if kv_transposed else 0
    u_ref, va_ref, qs_ref, kb_ref, vb_ref, k_ref, v_ref = refs[n_extra + n_stacks:n_extra + n_stacks + 7]
    h = _rmsnorm(x_ref[...], ln_ref[...]).astype(BF16)

    def proj(j):
        return jnp.dot(h, w_ref[:, j * width:(j + 1) * width], preferred_element_type=F32)

    u_ref[...] = jax.nn.gelu(proj(0))
    va = jax.nn.gelu(proj(1))
    e = width // SGU_GROUPS
    for g in range(SGU_GROUPS):
        va_ref[:, g * e:(g + 1) * e] = _rmsnorm(va[:, g * e:(g + 1) * e], gv_ref[g:g + 1, :])
    qs_ref[...] = (proj(2) * q_scale).astype(BF16)
    if kv_transposed:
        kvt = lax.dot_general(refs[0][...], h, (((1,), (1,)), ((), ())), preferred_element_type=F32)
        k, v = kvt[:width], kvt[width:]
    else:
        k, v = proj(3), proj(4)
    k_ref[...] = k
    kb_ref[...] = k.astype(BF16)
    v_ref[...] = v
    vb_ref[...] = v.astype(BF16)


def _in_proj(x, ln, w_in, w_kvt, g_v, k_stack, v_stack, va_stack, layer, tm):
    n, d = x.shape
    width = d // 2
    in_w = w_in.shape[-1]
    row = lambda i: (i, 0)
    lay3 = lambda i: (layer, 0, 0)
    kv_transposed = w_kvt is not None
    f32_out = jax.ShapeDtypeStruct((n, width), F32)
    bf_out = jax.ShapeDtypeStruct((n, width), BF16)
    out_spec = pl.BlockSpec((tm, width), row)
    if kv_transposed:
        _, batch, _, t_len = k_stack.shape
        tpb = t_len // tm
        extra, extra_specs = [w_kvt], [_resident((None, 2 * width, d), lay3)]
        stacks = [k_stack, v_stack]
        stack_spec = pl.BlockSpec((None, None, width, tm), lambda i: (layer, i // tpb, 0, i % tpb))
        kb_shape = jax.ShapeDtypeStruct((batch, width, t_len), BF16)
        kb_spec = pl.BlockSpec((None, width, tm), lambda i: (i // tpb, 0, i % tpb))
        va_shape = jax.ShapeDtypeStruct((1, n, width), F32)
        va_spec = pl.BlockSpec((None, tm, width), lambda i: (0, i, 0))
        aliases = {5: 5, 6: 6}
    else:
        extra, extra_specs = [], []
        stacks = [k_stack, v_stack, va_stack]
        stack_spec = pl.BlockSpec((None, tm, width), lambda i: (layer, i, 0))
        kb_shape, kb_spec = bf_out, out_spec
        va_shape, va_spec = jax.ShapeDtypeStruct(va_stack.shape, F32), stack_spec
        aliases = {4: 5, 5: 6, 6: 1}
    return pl.pallas_call(
        functools.partial(_in_proj_body, width=width, q_scale=(width // SB_HEADS) ** -0.5, n_stacks=len(stacks),
                          kv_transposed=kv_transposed),
        grid=(n // tm,),
        in_specs=[pl.BlockSpec((tm, d), row),
                  _resident((None, 1, d), lay3),
                  _resident((None, d, in_w), lay3),
                  _resident((None, SGU_GROUPS, width // SGU_GROUPS), lay3)]
                 + extra_specs + [pl.BlockSpec(memory_space=pl.ANY)] * len(stacks),
        out_specs=[out_spec, va_spec, out_spec, kb_spec, kb_spec, stack_spec, stack_spec],
        out_shape=[f32_out, va_shape, bf_out, kb_shape, kb_shape,
                   jax.ShapeDtypeStruct(k_stack.shape, F32), jax.ShapeDtypeStruct(v_stack.shape, F32)],
        input_output_aliases=aliases,
        compiler_params=_params(1),
        name="in_proj",
    )(x, ln, w_in, g_v, *extra, *stacks)


def _neg_log2_keep(z2):
    neg_abs = lax.bitcast_convert_type(lax.bitcast_convert_type(z2, jnp.int32) | jnp.int32(-2 ** 31), F32)
    return jnp.maximum(z2, 0.0) + jnp.log2(1.0 + jnp.exp2(neg_abs))


def _neg_cumsum_matrix():
    r = lax.broadcasted_iota(jnp.int32, (2 * LANES, 2 * LANES), 0) % LANES
    c = lax.broadcasted_iota(jnp.int32, (2 * LANES, 2 * LANES), 1)
    return jnp.where((c >= LANES) | (r >= c), -1.0, 0.0).astype(BF16)


def _sb_prompt_body(q_ref, kt_ref, vt_ref, o_ref, nuo_ref, qm_ref, z2_ref, l_ref, a_ref, carry_ref, acc_ref, *, dh):
    t_len, gw = q_ref.shape
    tile = SB_TILE
    n_heads = gw // dh
    rows = n_heads * tile
    half_rows = rows // SB_SPLIT
    slabs = half_rows // LANES
    nuo_ref[...] = _neg_cumsum_matrix()
    lane_head = lax.broadcasted_iota(jnp.int32, (tile, gw), 1) // dh

    def fold(c, masked):
        k0 = pl.multiple_of(c * tile, tile)
        kt_c = kt_ref[:, pl.ds(k0, tile)]
        vt_c = vt_ref[:, pl.ds(k0, tile)]
        for half in range(SB_SPLIT):
            base = half * half_rows
            z_all = jnp.dot(qm_ref[base:base + half_rows, :], kt_c, preferred_element_type=F32)
            for sl in range(slabs):
                r0 = base + sl * LANES
                z2 = z_all[sl * LANES:(sl + 1) * LANES] * LOG2E
                nlk = _neg_log2_keep(z2)
                if masked:
                    qpos = (r0 % tile) + lax.broadcasted_iota(jnp.int32, (LANES, tile), 0)
                    kpos = lax.broadcasted_iota(jnp.int32, (LANES, tile), 1)
                    nlk = jnp.where(kpos < qpos, nlk, 0.0)
                z2_ref[r0:r0 + LANES, :] = z2
                hi = nlk.astype(BF16)
                lo = (nlk - hi.astype(F32)).astype(BF16)
                for s in range(tile // LANES):
                    blk = slice(s * LANES, (s + 1) * LANES)
                    l_ref[s, r0:r0 + LANES, :] = jnp.concatenate([hi[:, blk], lo[:, blk]], axis=1)
            l_all = jnp.concatenate([l_ref[1, base:base + half_rows, :], l_ref[0, base:base + half_rows, :]], axis=0)
            r_all = jnp.dot(l_all, nuo_ref[...], preferred_element_type=F32)
            for sl in range(slabs):
                r0 = base + sl * LANES
                r1 = r_all[sl * LANES:(sl + 1) * LANES]
                rz = r_all[half_rows + sl * LANES:half_rows + (sl + 1) * LANES]
                carry = carry_ref[r0:r0 + LANES, :]
                cum1 = r1[:, :LANES] + carry
                carry = carry + r1[:, LANES:]
                cum0 = rz[:, :LANES] + carry
                carry_ref[r0:r0 + LANES, :] = carry + rz[:, LANES:]
                a = jnp.exp2(z2_ref[r0:r0 + LANES, :] + jnp.concatenate([cum0, cum1], axis=1))
                if masked:
                    qpos = (r0 % tile) + lax.broadcasted_iota(jnp.int32, (LANES, tile), 0)
                    kpos = lax.broadcasted_iota(jnp.int32, (LANES, tile), 1)
                    a = jnp.where(kpos < qpos, a, 0.0)
                a_ref[r0:r0 + LANES, :] = a.astype(BF16)
            acc_ref[base:base + half_rows, :] += lax.dot_general(
                a_ref[base:base + half_rows, :], vt_c, (((1,), (1,)), ((), ())), preferred_element_type=F32)

    def q_tile(j, _):
        q0 = pl.multiple_of(j * tile, tile)
        q = q_ref[pl.ds(q0, tile), :].astype(F32)
        for h in range(n_heads):
            qm_ref[h * tile:(h + 1) * tile, :] = jnp.where(lane_head == h, q, 0.0).astype(BF16)
        carry_ref[...] = jnp.zeros_like(carry_ref)
        acc_ref[...] = jnp.zeros_like(acc_ref)
        fold(j, True)

        def more(state):
            c, max_carry = state
            return (c >= 0) & (max_carry >= SB_EXIT_LOG2)

        def earlier(state):
            c, _ = state
            fold(c, False)
            return c - 1, jnp.max(carry_ref[...])

        lax.while_loop(more, earlier, (j - 1, jnp.max(carry_ref[...])))
        out = acc_ref[0:tile, :]
        for h in range(1, n_heads):
            out = jnp.where(lane_head == h, acc_ref[h * tile:(h + 1) * tile, :], out)
        o_ref[pl.ds(q0, tile), :] = out
        return 0

    lax.fori_loop(0, t_len // tile, q_tile, 0)


def _sb_prompt(qs, ktb, vtb, batch, t_len):
    n, width = qs.shape
    dh = width // SB_HEADS
    gw = SB_GROUP * dh
    groups = width // gw
    rows = SB_GROUP * SB_TILE
    blk = pl.BlockSpec((t_len, gw), lambda b, g: (b, g))
    blk_t = pl.BlockSpec((None, gw, t_len), lambda b, g: (b, g, 0))
    return pl.pallas_call(
        functools.partial(_sb_prompt_body, dh=dh),
        grid=(batch, groups),
        in_specs=[blk, blk_t, blk_t],
        out_specs=blk,
        out_shape=jax.ShapeDtypeStruct((n, width), F32),
        scratch_shapes=[pltpu.VMEM((2 * LANES, 2 * LANES), BF16),
                        pltpu.VMEM((rows, gw), BF16),
                        pltpu.VMEM((rows, SB_TILE), F32),
                        pltpu.VMEM((SB_TILE // LANES, rows, 2 * LANES), BF16),
                        pltpu.VMEM((rows, SB_TILE), BF16),
                        pltpu.VMEM((rows, LANES), F32),
                        pltpu.VMEM((rows, gw), F32)],
        compiler_params=_params(2),
        name="sb_prompt",
    )(qs, ktb, vtb)


def _sb_sample_body(q_ref, kc_ref, vc_ref, kn_ref, vn_ref, o_ref, kt_all, vt_all, *, past, n_new):
    width = q_ref.shape[1]
    dh = width // SB_HEADS
    rows = SB_HEADS * n_new
    n_keys = kt_all.shape[1]
    zeros = jnp.zeros((width, n_keys - past - n_new), F32)
    kt_all[:, 0:past] = kc_ref[...].astype(BF16)
    vt_all[:, 0:past] = vc_ref[...].astype(BF16)
    kt_all[:, past:] = jnp.concatenate([kn_ref[...].astype(F32).T, zeros], axis=1).astype(BF16)
    vt_all[:, past:] = jnp.concatenate([vn_ref[...].astype(F32).T, zeros], axis=1).astype(BF16)

    q_rep = jnp.concatenate([q_ref[...].astype(F32)] * SB_HEADS, axis=0)
    row_head = lax.broadcasted_iota(jnp.int32, (rows, width), 0) // n_new
    col_head = lax.broadcasted_iota(jnp.int32, (rows, width), 1) // dh
    own = row_head == col_head
    q_bd = jnp.where(own, q_rep, 0.0).astype(BF16)

    z = jnp.dot(q_bd, kt_all[...], preferred_element_type=F32)
    kpos = lax.broadcasted_iota(jnp.int32, (rows, n_keys), 1)
    qpos = past + lax.broadcasted_iota(jnp.int32, (rows, n_keys), 0) % n_new
    valid = kpos < qpos
    lk = jnp.where(valid, _log_keep(z), 0.0)
    uo = _cumsum_matrix()
    n_blk = n_keys // LANES
    carry = jnp.zeros((rows, LANES), F32)
    cums = [None] * n_blk
    for s in reversed(range(n_blk)):
        cums[s], carry = _block_suffix_sums(lk[:, s * LANES:(s + 1) * LANES], uo, carry)
    a = jnp.where(valid, jnp.exp(z + jnp.concatenate(cums, axis=1)), 0.0)
    o_all = lax.dot_general(a.astype(BF16), vt_all[...], (((1,), (1,)), ((), ())),
                            preferred_element_type=F32)
    o_all = jnp.where(own, o_all, 0.0)
    out = o_all[0:n_new, :]
    for h in range(1, SB_HEADS):
        out = out + o_all[h * n_new:(h + 1) * n_new, :]
    o_ref[...] = out


def _sb_sample(qs, kb, vb, cache_kt, cache_vt, layer, n_new):
    n, width = qs.shape
    past = cache_kt.shape[3]
    assert past % LANES == 0 and n_new % 8 == 0 and n_new <= LANES
    n_keys = past + LANES
    row = lambda b: (b, 0)
    cache = lambda b: (layer, b, 0, 0)
    return pl.pallas_call(
        functools.partial(_sb_sample_body, past=past, n_new=n_new),
        grid=(n // n_new,),
        in_specs=[pl.BlockSpec((n_new, width), row),
                  pl.BlockSpec((None, None, width, past), cache),
                  pl.BlockSpec((None, None, width, past), cache),
                  pl.BlockSpec((n_new, width), row),
                  pl.BlockSpec((n_new, width), row)],
        out_specs=pl.BlockSpec((n_new, width), row),
        out_shape=jax.ShapeDtypeStruct((n, width), F32),
        scratch_shapes=[pltpu.VMEM((width, n_keys), BF16), pltpu.VMEM((width, n_keys), BF16)],
        compiler_params=_params(1),
        name="sb_sample",
    )(qs, cache_kt, cache_vt, kb, vb)


def _merge_body(x_ref, u_ref, va_ref, sb_ref, ws_ref, bs_ref, ga_ref, gb_ref, wo_ref, o_ref, cat_ref):
    tm = x_ref.shape[0]
    width = u_ref.shape[1]
    e = width // SGU_GROUPS
    r = lax.broadcasted_iota(jnp.int32, (SGU_CHUNK, SGU_CHUNK), 0)
    c = lax.broadcasted_iota(jnp.int32, (SGU_CHUNK, SGU_CHUNK), 1)
    w_tril = [jnp.where(r >= c, ws_ref[g], 0.0).astype(BF16) for g in range(SGU_GROUPS)]
    for ch in range(tm // SGU_CHUNK):
        rows = slice(ch * SGU_CHUNK, (ch + 1) * SGU_CHUNK)
        va = va_ref[rows, :].astype(BF16)
        mix = jnp.concatenate(
            [jnp.dot(w_tril[g], va[:, g * e:(g + 1) * e], preferred_element_type=F32) + bs_ref[:, g:g + 1]
             for g in range(SGU_GROUPS)], axis=1)
        a_out = u_ref[rows, :] * mix
        cat_ref[rows, 0:width] = _rmsnorm(a_out, ga_ref[...]).astype(BF16)
        cat_ref[rows, width:] = _rmsnorm(sb_ref[rows, :], gb_ref[...]).astype(BF16)
    o_ref[...] = x_ref[...] + jnp.dot(cat_ref[...], wo_ref[...], preferred_element_type=F32)


def _merge(x, u, va, va_layer, sb, w_mix, b_mix, g_a, g_b, w_out, layer, tm):
    n, d = x.shape
    width = u.shape[1]
    sb_width = sb.shape[1]
    row = lambda i: (i, 0)
    fix2 = lambda i: (0, 0)
    fix3 = lambda i: (0, 0, 0)
    lay3 = lambda i: (layer, 0, 0)
    return pl.pallas_call(
        _merge_body,
        grid=(n // tm,),
        in_specs=[pl.BlockSpec((tm, d), row),
                  pl.BlockSpec((tm, width), row),
                  pl.BlockSpec((None, tm, width), lambda i: (va_layer, i, 0)),
                  pl.BlockSpec((tm, sb_width), row),
                  _resident((SGU_GROUPS, SGU_CHUNK, SGU_CHUNK), fix3),
                  _resident((SGU_CHUNK, SGU_GROUPS), fix2),
                  _resident((None, 1, width), lay3),
                  _resident((None, 1, sb_width), lay3),
                  _resident((None, width + sb_width, d), lay3)],
        out_specs=pl.BlockSpec((tm, d), row),
        out_shape=jax.ShapeDtypeStruct((n, d), F32),
        scratch_shapes=[pltpu.VMEM((tm, width + sb_width), BF16)],
        compiler_params=_params(1),
        name="merge",
    )(x, u, va, sb, w_mix, b_mix, g_a, g_b, w_out)


def _mem_kv_body(m_ref, wk_ref, wv_ref, k_ref, v_ref):
    m = m_ref[...].astype(BF16)
    k_ref[...] = jnp.dot(m, wk_ref[...], preferred_element_type=F32)
    v_ref[...] = jnp.dot(m, wv_ref[...], preferred_element_type=F32)


def _mem_kv(mem, w_mk, w_mv):
    b, m, d = mem.shape
    depth = w_mk.shape[0]
    out = jax.ShapeDtypeStruct((depth, b, m, d), F32)
    w_spec = pl.BlockSpec((None, d, d), lambda l, i: (l, 0, 0))
    o_spec = pl.BlockSpec((None, None, m, d), lambda l, i: (l, i, 0, 0))
    return pl.pallas_call(
        _mem_kv_body,
        grid=(depth, b),
        in_specs=[pl.BlockSpec((None, m, d), lambda l, i: (i, 0, 0)), w_spec, w_spec],
        out_specs=[o_spec, o_spec],
        out_shape=[out, out],
        compiler_params=_params(2),
        name="mem_kv",
    )(mem, w_mk, w_mv)


def _mem_attn_body(x_ref, ln_ref, wq_ref, mk_ref, mv_ref, wo_ref, o_ref, oc_ref, *, n_streams):
    tm, d = x_ref.shape
    dh = d // MEM_HEADS
    rpb = tm // n_streams
    x = x_ref[...]
    h = _rmsnorm(x, ln_ref[...]).astype(BF16)
    q = (jnp.dot(h, wq_ref[...], preferred_element_type=F32) * dh ** -0.5).astype(BF16)
    for s in range(n_streams):
        rows = slice(s * rpb, (s + 1) * rpb)
        for hd in range(MEM_HEADS):
            cols = slice(hd * dh, (hd + 1) * dh)
            k = mk_ref[s, :, cols].astype(BF16)
            v = mv_ref[s, :, cols].astype(BF16)
            sc = lax.dot_general(q[rows, cols], k, (((1,), (1,)), ((), ())), preferred_element_type=F32)
            p = jnp.exp(sc - jnp.max(sc, axis=-1, keepdims=True))
            o = jnp.dot(p.astype(BF16), v, preferred_element_type=F32) / jnp.sum(p, axis=-1, keepdims=True)
            oc_ref[rows, cols] = o.astype(BF16)
    o_ref[...] = x + jnp.dot(oc_ref[...], wo_ref[...], preferred_element_type=F32)


def _mem_attn(x, ln, w_mq, mk, mv, w_mo, layer, tm, rows_per_stream):
    n, d = x.shape
    m = mk.shape[2]
    n_streams = max(1, tm // rows_per_stream)
    tiles_per_stream = max(1, rows_per_stream // tm)
    row = lambda i: (i, 0)
    lay3 = lambda i: (layer, 0, 0)
    mem = lambda i: (layer, i // tiles_per_stream, 0, 0)
    return pl.pallas_call(
        functools.partial(_mem_attn_body, n_streams=n_streams),
        grid=(n // tm,),
        in_specs=[pl.BlockSpec((tm, d), row),
                  _resident((None, 1, d), lay3),
                  _resident((None, d, d), lay3),
                  pl.BlockSpec((None, n_streams, m, d), mem),
                  pl.BlockSpec((None, n_streams, m, d), mem),
                  _resident((None, d, d), lay3)],
        out_specs=pl.BlockSpec((tm, d), row),
        out_shape=jax.ShapeDtypeStruct((n, d), F32),
        scratch_shapes=[pltpu.VMEM((tm, d), BF16)],
        compiler_params=_params(1),
        name="mem_attn",
    )(x, ln, w_mq, mk, mv, w_mo)


def _ffn_body(x_ref, ln_ref, wg_ref, wu_ref, wd_ref, lnf_ref, o_ref, *, final_norm):
    x = x_ref[...]
    h = _rmsnorm(x, ln_ref[...]).astype(BF16)
    gate = jnp.dot(h, wg_ref[...], preferred_element_type=F32)
    up = jnp.dot(h, wu_ref[...], preferred_element_type=F32)
    act = (jax.nn.silu(gate) * up).astype(BF16)
    y = x + jnp.dot(act, wd_ref[...], preferred_element_type=F32)
    if final_norm:
        y = _rmsnorm(y, lnf_ref[...])
    o_ref[...] = y


def _ffn(x, ln, w_gate, w_up, w_down, ln_final, layer, tm, final_norm):
    n, d = x.shape
    d_ff = w_gate.shape[-1]
    row = lambda i: (i, 0)
    lay3 = lambda i: (layer, 0, 0)
    return pl.pallas_call(
        functools.partial(_ffn_body, final_norm=final_norm),
        grid=(n // tm,),
        in_specs=[pl.BlockSpec((tm, d), row),
                  _resident((None, 1, d), lay3),
                  _resident((None, d, d_ff), lay3),
                  _resident((None, d, d_ff), lay3),
                  _resident((None, d_ff, d), lay3),
                  _resident((1, d), lambda i: (0, 0))],
        out_specs=pl.BlockSpec((tm, d), row),
        out_shape=jax.ShapeDtypeStruct((n, d), F32),
        compiler_params=_params(1),
        name="ffn",
    )(x, ln, w_gate, w_up, w_down, ln_final)


def kernel(x_prompt, x_sample, cache_sb_k, cache_sb_v, cache_mem_k, cache_mem_v, mem_prompt, ln_mix, w_in, g_sgu_v, w_sgu, b_sgu, g_out_sgu, g_out_sb, w_out, ln_mem, w_mq, w_mk, w_mv, w_mo, ln_ffn, w_ffn_gate, w_ffn_up, w_ffn_down, ln_final):
    batch, t_len, d = x_prompt.shape
    bs, n_new, _ = x_sample.shape
    depth = w_in.shape[0]
    past = cache_sb_k.shape[2]
    n_mem = mem_prompt.shape[1]
    width = d // 2
    dh = width // SB_HEADS
    e = width // SGU_GROUPS
    assert t_len % ROW_TILE == 0 and t_len % SB_TILE == 0 and SGU_CHUNK % n_new == 0

    w_in_b, w_out_b = w_in.astype(BF16), w_out.astype(BF16)
    w_kvt_b = jnp.swapaxes(w_in_b[:, :, 3 * width:], 1, 2)
    w_mq_b, w_mk_b, w_mv_b, w_mo_b = (w.astype(BF16) for w in (w_mq, w_mk, w_mv, w_mo))
    w_g_b, w_u_b, w_d_b = (w.astype(BF16) for w in (w_ffn_gate, w_ffn_up, w_ffn_down))
    as_rows = lambda g: g.reshape(depth, 1, -1)
    ln_mix_r, ln_mem_r, ln_ffn_r = as_rows(ln_mix), as_rows(ln_mem), as_rows(ln_ffn)
    g_a_r, g_b_r = as_rows(g_out_sgu), as_rows(g_out_sb)
    ln_final_r = ln_final.reshape(1, d)

    cache_kt = cache_sb_k.transpose(0, 1, 3, 4, 2).reshape(depth, bs, width, past)
    cache_vt = cache_sb_v.transpose(0, 1, 3, 4, 2).reshape(depth, bs, width, past)
    cmem_k = cache_mem_k.reshape(depth, bs, n_mem, d)
    cmem_v = cache_mem_v.reshape(depth, bs, n_mem, d)

    mk_p, mv_p = _mem_kv(mem_prompt, w_mk_b, w_mv_b)

    n_p = batch * t_len
    x = x_prompt.reshape(n_p, d)
    kt_st = jnp.zeros((depth, batch, width, t_len), F32)
    vt_st = jnp.zeros((depth, batch, width, t_len), F32)
    for l in range(depth):
        u, va, qs, ktb, vtb, kt_st, vt_st = _in_proj(x, ln_mix_r, w_in_b, w_kvt_b, g_sgu_v, kt_st, vt_st, None, l, ROW_TILE)
        sb = _sb_prompt(qs, ktb, vtb, batch, t_len)
        x = _merge(x, u, va, 0, sb, w_sgu[l], b_sgu[l].T, g_a_r, g_b_r, w_out_b, l, ROW_TILE)
        x = _mem_attn(x, ln_mem_r, w_mq_b, mk_p, mv_p, w_mo_b, l, ROW_TILE, t_len)
        x = _ffn(x, ln_ffn_r, w_g_b, w_u_b, w_d_b, ln_final_r, l, ROW_TILE, l == depth - 1)
    y_prompt = x.reshape(batch, t_len, d)
    sb_k_prompt = kt_st.reshape(depth, batch, SB_HEADS, dh, t_len).transpose(0, 1, 4, 2, 3)
    sb_v_prompt = vt_st.reshape(depth, batch, SB_HEADS, dh, t_len).transpose(0, 1, 4, 2, 3)

    n_s = bs * n_new
    tm_s = min(ROW_TILE, n_s)
    tm_mem = 4 * n_new
    reps = SGU_CHUNK // n_new
    x = x_sample.reshape(n_s, d)
    k_st = jnp.zeros((depth, n_s, width), F32)
    v_st = jnp.zeros((depth, n_s, width), F32)
    va_st = jnp.zeros((depth, n_s, width), F32)
    for l in range(depth):
        u, va_st, qs, kb, vb, k_st, v_st = _in_proj(x, ln_mix_r, w_in_b, None, g_sgu_v, k_st, v_st, va_st, l, tm_s)
        sb = _sb_sample(qs, kb, vb, cache_kt, cache_vt, l, n_new)
        w_blk = jnp.einsum('ab,gts->gatbs', jnp.eye(reps, dtype=F32), w_sgu[l][:, :n_new, :n_new])
        w_blk = w_blk.reshape(SGU_GROUPS, SGU_CHUNK, SGU_CHUNK)
        b_blk = jnp.tile(b_sgu[l][:, :n_new], (1, reps)).T
        x = _merge(x, u, va_st, l, sb, w_blk, b_blk, g_a_r, g_b_r, w_out_b, l, tm_s)
        x = _mem_attn(x, ln_mem_r, w_mq_b, cmem_k, cmem_v, w_mo_b, l, tm_mem, n_new)
        x = _ffn(x, ln_ffn_r, w_g_b, w_u_b, w_d_b, ln_final_r, l, tm_s, l == depth - 1)
    y_sample = x.reshape(bs, n_new, d)

    return (y_prompt, y_sample, sb_k_prompt, sb_v_prompt,
            mk_p.reshape(depth, batch, n_mem, MEM_HEADS, d // MEM_HEADS),
            mv_p.reshape(depth, batch, n_mem, MEM_HEADS, d // MEM_HEADS),
            k_st.reshape(depth, bs, n_new, SB_HEADS, dh),
            v_st.reshape(depth, bs, n_new, SB_HEADS, dh),
            va_st.reshape(depth, bs, n_new, SGU_GROUPS, e))
```

```python
import functools

import jax
import jax.numpy as jnp
from jax import lax
from jax.experimental import pallas as pl
from jax.experimental.pallas import tpu as pltpu

F32 = jnp.float32
BF16 = jnp.bfloat16
EPS = 1e-6

LANES = 128
SGU_CHUNK = 128
SGU_GROUPS = 4
SB_HEADS = 8
MEM_HEADS = 4
ROW_TILE = 512
SB_TILE = 256
SB_GROUP = 4
SB_SPLIT = 4
LOG2E = 1.4426950408889634
SB_EXIT_LOG = -106.0
VMEM_LIMIT = 56 * 1024 * 1024


def _params(n_axes):
    return pltpu.CompilerParams(dimension_semantics=("parallel",) * n_axes,
                                vmem_limit_bytes=VMEM_LIMIT)


def _resident(block_shape, index_map):
    return pl.BlockSpec(block_shape, index_map, pipeline_mode=pl.Buffered(1))


def _rmsnorm(x, g):
    return x * lax.rsqrt(jnp.mean(x * x, axis=-1, keepdims=True) + EPS) * g


def _log_keep(z):
    return -(jnp.maximum(z, 0.0) + jnp.log(1.0 + jnp.exp(-jnp.abs(z))))


def _cumsum_matrix():
    r = lax.broadcasted_iota(jnp.int32, (2 * LANES, 2 * LANES), 0) % LANES
    c = lax.broadcasted_iota(jnp.int32, (2 * LANES, 2 * LANES), 1)
    return jnp.where((c >= LANES) | (r >= c), 1.0, 0.0).astype(BF16)


def _block_suffix_sums(lk_blk, uo, carry):
    hi = lk_blk.astype(BF16)
    lo = (lk_blk - hi.astype(F32)).astype(BF16)
    r = jnp.dot(jnp.concatenate([hi, lo], axis=1), uo, preferred_element_type=F32)
    return r[:, :LANES] + carry, carry + r[:, LANES:]


def _in_proj_body(x_ref, ln_ref, w_ref, gv_ref, *refs, width, q_scale, n_stacks, kv_transposed):
    n_extra = 1 if kv_transposed else 0
    u_ref, va_ref, qs_ref, kb_ref, vb_ref, k_ref, v_ref = refs[n_extra + n_stacks:n_extra + n_stacks + 7]
    h = _rmsnorm(x_ref[...], ln_ref[...]).astype(BF16)

    def proj(j):
        return jnp.dot(h, w_ref[:, j * width:(j + 1) * width], preferred_element_type=F32)

    u_ref[...] = jax.nn.gelu(proj(0))
    va = jax.nn.gelu(proj(1))
    e = width // SGU_GROUPS
    for g in range(SGU_GROUPS):
        va_ref[:, g * e:(g + 1) * e] = _rmsnorm(va[:, g * e:(g + 1) * e], gv_ref[g:g + 1, :])
    qs_ref[...] = (proj(2) * q_scale).astype(BF16)
    if kv_transposed:
        kvt = lax.dot_general(refs[0][...], h, (((1,), (1,)), ((), ())), preferred_element_type=F32)
        k, v = kvt[:width], kvt[width:]
    else:
        k, v = proj(3), proj(4)
    k_ref[...] = k
    kb_ref[...] = k.astype(BF16)
    v_ref[...] = v
    vb_ref[...] = v.astype(BF16)


def _in_proj(x, ln, w_in, w_kvt, g_v, k_stack, v_stack, va_stack, layer, tm):
    n, d = x.shape
    width = d // 2
    in_w = w_in.shape[-1]
    row = lambda i: (i, 0)
    lay3 = lambda i: (layer, 0, 0)
    kv_transposed = w_kvt is not None
    f32_out = jax.ShapeDtypeStruct((n, width), F32)
    bf_out = jax.ShapeDtypeStruct((n, width), BF16)
    out_spec = pl.BlockSpec((tm, width), row)
    if kv_transposed:
        _, batch, _, t_len = k_stack.shape
        tpb = t_len // tm
        extra, extra_specs = [w_kvt], [_resident((None, 2 * width, d), lay3)]
        stacks = [k_stack, v_stack]
        stack_spec = pl.BlockSpec((None, None, width, tm), lambda i: (layer, i // tpb, 0, i % tpb))
        kb_shape = jax.ShapeDtypeStruct((batch, width, t_len), BF16)
        kb_spec = pl.BlockSpec((None, width, tm), lambda i: (i // tpb, 0, i % tpb))
        va_shape = jax.ShapeDtypeStruct((1, n, width), F32)
        va_spec = pl.BlockSpec((None, tm, width), lambda i: (0, i, 0))
        aliases = {5: 5, 6: 6}
    else:
        extra, extra_specs = [], []
        stacks = [k_stack, v_stack, va_stack]
        stack_spec = pl.BlockSpec((None, tm, width), lambda i: (layer, i, 0))
        kb_shape, kb_spec = bf_out, out_spec
        va_shape, va_spec = jax.ShapeDtypeStruct(va_stack.shape, F32), stack_spec
        aliases = {4: 5, 5: 6, 6: 1}
    return pl.pallas_call(
        functools.partial(_in_proj_body, width=width, q_scale=(width // SB_HEADS) ** -0.5, n_stacks=len(stacks),
                          kv_transposed=kv_transposed),
        grid=(n // tm,),
        in_specs=[pl.BlockSpec((tm, d), row),
                  _resident((None, 1, d), lay3),
                  _resident((None, d, in_w), lay3),
                  _resident((None, SGU_GROUPS, width // SGU_GROUPS), lay3)]
                 + extra_specs + [pl.BlockSpec(memory_space=pl.ANY)] * len(stacks),
        out_specs=[out_spec, va_spec, out_spec, kb_spec, kb_spec, stack_spec, stack_spec],
        out_shape=[f32_out, va_shape, bf_out, kb_shape, kb_shape,
                   jax.ShapeDtypeStruct(k_stack.shape, F32), jax.ShapeDtypeStruct(v_stack.shape, F32)],
        input_output_aliases=aliases,
        compiler_params=_params(1),
        name="in_proj",
    )(x, ln, w_in, g_v, *extra, *stacks)


def _neg_log_keep(z):
    return jnp.maximum(z, 0.0) + jnp.log(1.0 + jnp.exp2(jnp.abs(z) * (-LOG2E)))


def _neg_cumsum_matrix():
    r = lax.broadcasted_iota(jnp.int32, (2 * LANES, 2 * LANES), 0) % LANES
    c = lax.broadcasted_iota(jnp.int32, (2 * LANES, 2 * LANES), 1)
    return jnp.where((c >= LANES) | (r >= c), -1.0, 0.0).astype(BF16)


def _sb_prompt_body(q_ref, kt_ref, vt_ref, o_ref, nuo_ref, qm_ref, z_ref, l_ref, a_ref, carry_ref, acc_ref, *, dh):
    t_len, gw = q_ref.shape
    tile = SB_TILE
    n_heads = gw // dh
    rows = n_heads * tile
    part_rows = rows // SB_SPLIT
    slabs = part_rows // LANES
    nuo_ref[...] = _neg_cumsum_matrix()
    lane_head = lax.broadcasted_iota(jnp.int32, (tile, gw), 1) // dh

    def stage_logits(c, masked, slot, part):
        k0 = pl.multiple_of(c * tile, tile)
        base = part * part_rows
        return jnp.dot(qm_ref[base:base + part_rows, :], kt_ref[:, pl.ds(k0, tile)], preferred_element_type=F32)

    def stage_keep(c, masked, slot, part, z_all):
        base = part * part_rows
        for sl in range(slabs):
            r0 = base + sl * LANES
            z = z_all[sl * LANES:(sl + 1) * LANES]
            nlk = _neg_log_keep(z)
            if masked:
                qpos = (r0 % tile) + lax.broadcasted_iota(jnp.int32, (LANES, tile), 0)
                kpos = lax.broadcasted_iota(jnp.int32, (LANES, tile), 1)
                nlk = jnp.where(kpos < qpos, nlk, 0.0)
            z_ref[slot, r0:r0 + LANES, :] = z
            hi = nlk.astype(BF16)
            lo = (nlk - hi.astype(F32)).astype(BF16)
            for s in range(tile // LANES):
                blk = slice(s * LANES, (s + 1) * LANES)
                l_ref[slot, s, r0:r0 + LANES, :] = jnp.concatenate([hi[:, blk], lo[:, blk]], axis=1)
        l_all = jnp.concatenate([l_ref[slot, 1, base:base + part_rows, :],
                                 l_ref[slot, 0, base:base + part_rows, :]], axis=0)
        return jnp.dot(l_all, nuo_ref[...], preferred_element_type=F32)

    def stage_weights(c, masked, slot, part, r_all):
        k0 = pl.multiple_of(c * tile, tile)
        base = part * part_rows
        for sl in range(slabs):
            r0 = base + sl * LANES
            r1 = r_all[sl * LANES:(sl + 1) * LANES]
            rz = r_all[part_rows + sl * LANES:part_rows + (sl + 1) * LANES]
            carry = carry_ref[r0:r0 + LANES, :]
            cum1 = r1[:, :LANES] + carry
            carry = carry + r1[:, LANES:]
            cum0 = rz[:, :LANES] + carry
            carry_ref[r0:r0 + LANES, :] = carry + rz[:, LANES:]
            a = jnp.exp(z_ref[slot, r0:r0 + LANES, :] + jnp.concatenate([cum0, cum1], axis=1))
            if masked:
                qpos = (r0 % tile) + lax.broadcasted_iota(jnp.int32, (LANES, tile), 0)
                kpos = lax.broadcasted_iota(jnp.int32, (LANES, tile), 1)
                a = jnp.where(kpos < qpos, a, 0.0)
            a_ref[slot, r0:r0 + LANES, :] = a.astype(BF16)
        acc_ref[base:base + part_rows, :] += lax.dot_general(
            a_ref[slot, base:base + part_rows, :], vt_ref[:, pl.ds(k0, tile)], (((1,), (1,)), ((), ())),
            preferred_element_type=F32)

    def fold(*tiles):
        items = [(c, masked, slot, part) for (c, masked, slot) in tiles for part in range(SB_SPLIT)]
        logits, sums = {}, {}
        for step in range(len(items) + 2):
            if step < len(items):
                logits[step] = stage_logits(*items[step])
            if 0 <= step - 1 < len(items):
                sums[step - 1] = stage_keep(*items[step - 1], logits.pop(step - 1))
            if 0 <= step - 2 < len(items):
                stage_weights(*items[step - 2], sums.pop(step - 2))

    def begin(q0):
        q = q_ref[pl.ds(q0, tile), :].astype(F32)
        for h in range(n_heads):
            qm_ref[h * tile:(h + 1) * tile, :] = jnp.where(lane_head == h, q, 0.0).astype(BF16)
        carry_ref[...] = jnp.zeros_like(carry_ref)
        acc_ref[...] = jnp.zeros_like(acc_ref)

    def finish(q0):
        out = acc_ref[0:tile, :]
        for h in range(1, n_heads):
            out = jnp.where(lane_head == h, acc_ref[h * tile:(h + 1) * tile, :], out)
        o_ref[pl.ds(q0, tile), :] = out

    begin(0)
    fold((0, True, 0))
    finish(0)

    def q_tile(j, _):
        q0 = pl.multiple_of(j * tile, tile)
        begin(q0)
        fold((j, True, 0), (j - 1, False, 1))

        def more(state):
            c, max_carry = state
            return (c >= 0) & (max_carry >= SB_EXIT_LOG)

        def earlier(state):
            c, _ = state
            fold((c, False, 0))
            return c - 1, jnp.max(carry_ref[...])

        lax.while_loop(more, earlier, (j - 2, jnp.max(carry_ref[...])))
        finish(q0)
        return 0

    lax.fori_loop(1, t_len // tile, q_tile, 0)


def _sb_prompt(qs, ktb, vtb, batch, t_len):
    n, width = qs.shape
    dh = width // SB_HEADS
    gw = SB_GROUP * dh
    groups = width // gw
    rows = SB_GROUP * SB_TILE
    blk = pl.BlockSpec((t_len, gw), lambda b, g: (b, g))
    blk_t = pl.BlockSpec((None, gw, t_len), lambda b, g: (b, g, 0))
    return pl.pallas_call(
        functools.partial(_sb_prompt_body, dh=dh),
        grid=(batch, groups),
        in_specs=[blk, blk_t, blk_t],
        out_specs=blk,
        out_shape=jax.ShapeDtypeStruct((n, width), F32),
        scratch_shapes=[pltpu.VMEM((2 * LANES, 2 * LANES), BF16),
                        pltpu.VMEM((rows, gw), BF16),
                        pltpu.VMEM((2, rows, SB_TILE), F32),
                        pltpu.VMEM((2, SB_TILE // LANES, rows, 2 * LANES), BF16),
                        pltpu.VMEM((2, rows, SB_TILE), BF16),
                        pltpu.VMEM((rows, LANES), F32),
                        pltpu.VMEM((rows, gw), F32)],
        compiler_params=_params(2),
        name="sb_prompt",
    )(qs, ktb, vtb)


def _sb_sample_body(q_ref, kc_ref, vc_ref, kn_ref, vn_ref, o_ref, kt_all, vt_all, *, past, n_new):
    width = q_ref.shape[1]
    dh = width // SB_HEADS
    rows = SB_HEADS * n_new
    n_keys = kt_all.shape[1]
    zeros = jnp.zeros((width, n_keys - past - n_new), F32)
    kt_all[:, 0:past] = kc_ref[...].astype(BF16)
    vt_all[:, 0:past] = vc_ref[...].astype(BF16)
    kt_all[:, past:] = jnp.concatenate([kn_ref[...].astype(F32).T, zeros], axis=1).astype(BF16)
    vt_all[:, past:] = jnp.concatenate([vn_ref[...].astype(F32).T, zeros], axis=1).astype(BF16)

    q_rep = jnp.concatenate([q_ref[...].astype(F32)] * SB_HEADS, axis=0)
    row_head = lax.broadcasted_iota(jnp.int32, (rows, width), 0) // n_new
    col_head = lax.broadcasted_iota(jnp.int32, (rows, width), 1) // dh
    own = row_head == col_head
    q_bd = jnp.where(own, q_rep, 0.0).astype(BF16)

    z = jnp.dot(q_bd, kt_all[...], preferred_element_type=F32)
    kpos = lax.broadcasted_iota(jnp.int32, (rows, n_keys), 1)
    qpos = past + lax.broadcasted_iota(jnp.int32, (rows, n_keys), 0) % n_new
    valid = kpos < qpos
    lk = jnp.where(valid, _log_keep(z), 0.0)
    uo = _cumsum_matrix()
    n_blk = n_keys // LANES
    carry = jnp.zeros((rows, LANES), F32)
    cums = [None] * n_blk
    for s in reversed(range(n_blk)):
        cums[s], carry = _block_suffix_sums(lk[:, s * LANES:(s + 1) * LANES], uo, carry)
    a = jnp.where(valid, jnp.exp(z + jnp.concatenate(cums, axis=1)), 0.0)
    o_all = lax.dot_general(a.astype(BF16), vt_all[...], (((1,), (1,)), ((), ())),
                            preferred_element_type=F32)
    o_all = jnp.where(own, o_all, 0.0)
    out = o_all[0:n_new, :]
    for h in range(1, SB_HEADS):
        out = out + o_all[h * n_new:(h + 1) * n_new, :]
    o_ref[...] = out


def _sb_sample(qs, kb, vb, cache_kt, cache_vt, layer, n_new):
    n, width = qs.shape
    past = cache_kt.shape[3]
    assert past % LANES == 0 and n_new % 8 == 0 and n_new <= LANES
    n_keys = past + LANES
    row = lambda b: (b, 0)
    cache = lambda b: (layer, b, 0, 0)
    return pl.pallas_call(
        functools.partial(_sb_sample_body, past=past, n_new=n_new),
        grid=(n // n_new,),
        in_specs=[pl.BlockSpec((n_new, width), row),
                  pl.BlockSpec((None, None, width, past), cache),
                  pl.BlockSpec((None, None, width, past), cache),
                  pl.BlockSpec((n_new, width), row),
                  pl.BlockSpec((n_new, width), row)],
        out_specs=pl.BlockSpec((n_new, width), row),
        out_shape=jax.ShapeDtypeStruct((n, width), F32),
        scratch_shapes=[pltpu.VMEM((width, n_keys), BF16), pltpu.VMEM((width, n_keys), BF16)],
        compiler_params=_params(1),
        name="sb_sample",
    )(qs, cache_kt, cache_vt, kb, vb)


def _merge_body(x_ref, u_ref, va_ref, sb_ref, ws_ref, bs_ref, ga_ref, gb_ref, wo_ref, o_ref, cat_ref):
    tm = x_ref.shape[0]
    width = u_ref.shape[1]
    e = width // SGU_GROUPS
    r = lax.broadcasted_iota(jnp.int32, (SGU_CHUNK, SGU_CHUNK), 0)
    c = lax.broadcasted_iota(jnp.int32, (SGU_CHUNK, SGU_CHUNK), 1)
    n_chunks = tm // SGU_CHUNK
    cat_ref[:, width:] = _rmsnorm(sb_ref[...], gb_ref[...]).astype(BF16)
    va = va_ref[...].astype(BF16)
    mixes = []
    for g in range(SGU_GROUPS):
        w_g = jnp.where(r >= c, ws_ref[g], 0.0).astype(BF16)
        chunks = jnp.concatenate([va[ch * SGU_CHUNK:(ch + 1) * SGU_CHUNK, g * e:(g + 1) * e]
                                  for ch in range(n_chunks)], axis=1)
        mixes.append(jnp.dot(w_g, chunks, preferred_element_type=F32) + bs_ref[:, g:g + 1])
    mix = jnp.concatenate([jnp.concatenate([m[:, ch * e:(ch + 1) * e] for m in mixes], axis=1)
                           for ch in range(n_chunks)], axis=0)
    cat_ref[:, 0:width] = _rmsnorm(u_ref[...] * mix, ga_ref[...]).astype(BF16)
    o_ref[...] = x_ref[...] + jnp.dot(cat_ref[...], wo_ref[...], preferred_element_type=F32)


def _merge(x, u, va, va_layer, sb, w_mix, b_mix, g_a, g_b, w_out, layer, tm):
    n, d = x.shape
    width = u.shape[1]
    sb_width = sb.shape[1]
    row = lambda i: (i, 0)
    fix2 = lambda i: (0, 0)
    fix3 = lambda i: (0, 0, 0)
    lay3 = lambda i: (layer, 0, 0)
    return pl.pallas_call(
        _merge_body,
        grid=(n // tm,),
        in_specs=[pl.BlockSpec((tm, d), row),
                  pl.BlockSpec((tm, width), row),
                  pl.BlockSpec((None, tm, width), lambda i: (va_layer, i, 0)),
                  pl.BlockSpec((tm, sb_width), row),
                  _resident((SGU_GROUPS, SGU_CHUNK, SGU_CHUNK), fix3),
                  _resident((SGU_CHUNK, SGU_GROUPS), fix2),
                  _resident((None, 1, width), lay3),
                  _resident((None, 1, sb_width), lay3),
                  _resident((None, width + sb_width, d), lay3)],
        out_specs=pl.BlockSpec((tm, d), row),
        out_shape=jax.ShapeDtypeStruct((n, d), F32),
        scratch_shapes=[pltpu.VMEM((tm, width + sb_width), BF16)],
        compiler_params=_params(1),
        name="merge",
    )(x, u, va, sb, w_mix, b_mix, g_a, g_b, w_out)


def _mem_kv_body(m_ref, wk_ref, wv_ref, k_ref, v_ref):
    m = m_ref[...].astype(BF16)
    k_ref[...] = jnp.dot(m, wk_ref[...], preferred_element_type=F32)
    v_ref[...] = jnp.dot(m, wv_ref[...], preferred_element_type=F32)


def _mem_kv(mem, w_mk, w_mv):
    b, m, d = mem.shape
    depth = w_mk.shape[0]
    out = jax.ShapeDtypeStruct((depth, b, m, d), F32)
    w_spec = pl.BlockSpec((None, d, d), lambda l, i: (l, 0, 0))
    o_spec = pl.BlockSpec((None, None, m, d), lambda l, i: (l, i, 0, 0))
    return pl.pallas_call(
        _mem_kv_body,
        grid=(depth, b),
        in_specs=[pl.BlockSpec((None, m, d), lambda l, i: (i, 0, 0)), w_spec, w_spec],
        out_specs=[o_spec, o_spec],
        out_shape=[out, out],
        compiler_params=_params(2),
        name="mem_kv",
    )(mem, w_mk, w_mv)


def _mem_attn_body(x_ref, ln_ref, wq_ref, mk_ref, mv_ref, wo_ref, o_ref, oc_ref, *, n_streams):
    tm, d = x_ref.shape
    dh = d // MEM_HEADS
    rpb = tm // n_streams
    x = x_ref[...]
    h = _rmsnorm(x, ln_ref[...]).astype(BF16)
    q = (jnp.dot(h, wq_ref[...], preferred_element_type=F32) * dh ** -0.5)
    nt_dims = (((1,), (1,)), ((), ()))
    if MEM_HEADS * rpb <= LANES:
        stacked = MEM_HEADS * rpb
        own = (lax.broadcasted_iota(jnp.int32, (stacked, d), 0) // rpb
               == lax.broadcasted_iota(jnp.int32, (stacked, d), 1) // dh)
        for s in range(n_streams):
            rows = slice(s * rpb, (s + 1) * rpb)
            q_bd = jnp.where(own, jnp.concatenate([q[rows]] * MEM_HEADS, axis=0), 0.0).astype(BF16)
            sc = lax.dot_general(q_bd, mk_ref[s].astype(BF16), nt_dims, preferred_element_type=F32)
            p = jnp.exp(sc - jnp.max(sc, axis=-1, keepdims=True))
            o = jnp.dot(p.astype(BF16), mv_ref[s].astype(BF16), preferred_element_type=F32)
            o = jnp.where(own, o / jnp.sum(p, axis=-1, keepdims=True), 0.0)
            o_s = o[0:rpb]
            for hd in range(1, MEM_HEADS):
                o_s = o_s + o[hd * rpb:(hd + 1) * rpb]
            oc_ref[rows, :] = o_s.astype(BF16)
        o_ref[...] = x + jnp.dot(oc_ref[...], wo_ref[...], preferred_element_type=F32)
        return
    q = q.astype(BF16)
    for s in range(n_streams):
        rows = slice(s * rpb, (s + 1) * rpb)
        for hd in range(MEM_HEADS):
            cols = slice(hd * dh, (hd + 1) * dh)
            k = mk_ref[s, :, cols].astype(BF16)
            v = mv_ref[s, :, cols].astype(BF16)
            sc = lax.dot_general(q[rows, cols], k, (((1,), (1,)), ((), ())), preferred_element_type=F32)
            p = jnp.exp(sc - jnp.max(sc, axis=-1, keepdims=True))
            o = jnp.dot(p.astype(BF16), v, preferred_element_type=F32) / jnp.sum(p, axis=-1, keepdims=True)
            oc_ref[rows, cols] = o.astype(BF16)
    o_ref[...] = x + jnp.dot(oc_ref[...], wo_ref[...], preferred_element_type=F32)


def _mem_attn(x, ln, w_mq, mk, mv, w_mo, layer, tm, rows_per_stream):
    n, d = x.shape
    m = mk.shape[2]
    n_streams = max(1, tm // rows_per_stream)
    tiles_per_stream = max(1, rows_per_stream // tm)
    row = lambda i: (i, 0)
    lay3 = lambda i: (layer, 0, 0)
    mem = lambda i: (layer, i // tiles_per_stream, 0, 0)
    return pl.pallas_call(
        functools.partial(_mem_attn_body, n_streams=n_streams),
        grid=(n // tm,),
        in_specs=[pl.BlockSpec((tm, d), row),
                  _resident((None, 1, d), lay3),
                  _resident((None, d, d), lay3),
                  pl.BlockSpec((None, n_streams, m, d), mem),
                  pl.BlockSpec((None, n_streams, m, d), mem),
                  _resident((None, d, d), lay3)],
        out_specs=pl.BlockSpec((tm, d), row),
        out_shape=jax.ShapeDtypeStruct((n, d), F32),
        scratch_shapes=[pltpu.VMEM((tm, d), BF16)],
        compiler_params=_params(1),
        name="mem_attn",
    )(x, ln, w_mq, mk, mv, w_mo)


def _ffn_body(x_ref, ln_ref, wg_ref, wu_ref, wd_ref, lnf_ref, o_ref, *, final_norm):
    x = x_ref[...]
    h = _rmsnorm(x, ln_ref[...]).astype(BF16)
    gate = jnp.dot(h, wg_ref[...], preferred_element_type=F32)
    up = jnp.dot(h, wu_ref[...], preferred_element_type=F32)
    act = (jax.nn.silu(gate) * up).astype(BF16)
    y = x + jnp.dot(act, wd_ref[...], preferred_element_type=F32)
    if final_norm:
        y = _rmsnorm(y, lnf_ref[...])
    o_ref[...] = y


def _ffn(x, ln, w_gate, w_up, w_down, ln_final, layer, tm, final_norm):
    n, d = x.shape
    d_ff = w_gate.shape[-1]
    row = lambda i: (i, 0)
    lay3 = lambda i: (layer, 0, 0)
    return pl.pallas_call(
        functools.partial(_ffn_body, final_norm=final_norm),
        grid=(n // tm,),
        in_specs=[pl.BlockSpec((tm, d), row),
                  _resident((None, 1, d), lay3),
                  _resident((None, d, d_ff), lay3),
                  _resident((None, d, d_ff), lay3),
                  _resident((None, d_ff, d), lay3),
                  _resident((1, d), lambda i: (0, 0))],
        out_specs=pl.BlockSpec((tm, d), row),
        out_shape=jax.ShapeDtypeStruct((n, d), F32),
        compiler_params=_params(1),
        name="ffn",
    )(x, ln, w_gate, w_up, w_down, ln_final)


def kernel(x_prompt, x_sample, cache_sb_k, cache_sb_v, cache_mem_k, cache_mem_v, mem_prompt, ln_mix, w_in, g_sgu_v, w_sgu, b_sgu, g_out_sgu, g_out_sb, w_out, ln_mem, w_mq, w_mk, w_mv, w_mo, ln_ffn, w_ffn_gate, w_ffn_up, w_ffn_down, ln_final):
    batch, t_len, d = x_prompt.shape
    bs, n_new, _ = x_sample.shape
    depth = w_in.shape[0]
    past = cache_sb_k.shape[2]
    n_mem = mem_prompt.shape[1]
    width = d // 2
    dh = width // SB_HEADS
    e = width // SGU_GROUPS
    assert t_len % ROW_TILE == 0 and t_len % SB_TILE == 0 and SGU_CHUNK % n_new == 0

    w_in_b, w_out_b = w_in.astype(BF16), w_out.astype(BF16)
    w_kvt_b = jnp.swapaxes(w_in_b[:, :, 3 * width:], 1, 2)
    w_mq_b, w_mk_b, w_mv_b, w_mo_b = (w.astype(BF16) for w in (w_mq, w_mk, w_mv, w_mo))
    w_g_b, w_u_b, w_d_b = (w.astype(BF16) for w in (w_ffn_gate, w_ffn_up, w_ffn_down))
    as_rows = lambda g: g.reshape(depth, 1, -1)
    ln_mix_r, ln_mem_r, ln_ffn_r = as_rows(ln_mix), as_rows(ln_mem), as_rows(ln_ffn)
    g_a_r, g_b_r = as_rows(g_out_sgu), as_rows(g_out_sb)
    ln_final_r = ln_final.reshape(1, d)

    cache_kt = cache_sb_k.transpose(0, 1, 3, 4, 2).reshape(depth, bs, width, past)
    cache_vt = cache_sb_v.transpose(0, 1, 3, 4, 2).reshape(depth, bs, width, past)
    cmem_k = cache_mem_k.reshape(depth, bs, n_mem, d)
    cmem_v = cache_mem_v.reshape(depth, bs, n_mem, d)

    mk_p, mv_p = _mem_kv(mem_prompt, w_mk_b, w_mv_b)

    n_p = batch * t_len
    x = x_prompt.reshape(n_p, d)
    kt_st = jnp.zeros((depth, batch, width, t_len), F32)
    vt_st = jnp.zeros((depth, batch, width, t_len), F32)
    for l in range(depth):
        u, va, qs, ktb, vtb, kt_st, vt_st = _in_proj(x, ln_mix_r, w_in_b, w_kvt_b, g_sgu_v, kt_st, vt_st, None, l, ROW_TILE)
        sb = _sb_prompt(qs, ktb, vtb, batch, t_len)
        x = _merge(x, u, va, 0, sb, w_sgu[l], b_sgu[l].T, g_a_r, g_b_r, w_out_b, l, ROW_TILE)
        x = _mem_attn(x, ln_mem_r, w_mq_b, mk_p, mv_p, w_mo_b, l, ROW_TILE, t_len)
        x = _ffn(x, ln_ffn_r, w_g_b, w_u_b, w_d_b, ln_final_r, l, ROW_TILE, l == depth - 1)
    y_prompt = x.reshape(batch, t_len, d)
    sb_k_prompt = kt_st.reshape(depth, batch, SB_HEADS, dh, t_len).transpose(0, 1, 4, 2, 3)
    sb_v_prompt = vt_st.reshape(depth, batch, SB_HEADS, dh, t_len).transpose(0, 1, 4, 2, 3)

    n_s = bs * n_new
    tm_s = min(ROW_TILE, n_s)
    tm_mem = 4 * n_new
    reps = SGU_CHUNK // n_new
    x = x_sample.reshape(n_s, d)
    k_st = jnp.zeros((depth, n_s, width), F32)
    v_st = jnp.zeros((depth, n_s, width), F32)
    va_st = jnp.zeros((depth, n_s, width), F32)
    for l in range(depth):
        u, va_st, qs, kb, vb, k_st, v_st = _in_proj(x, ln_mix_r, w_in_b, None, g_sgu_v, k_st, v_st, va_st, l, tm_s)
        sb = _sb_sample(qs, kb, vb, cache_kt, cache_vt, l, n_new)
        w_blk = jnp.einsum('ab,gts->gatbs', jnp.eye(reps, dtype=F32), w_sgu[l][:, :n_new, :n_new])
        w_blk = w_blk.reshape(SGU_GROUPS, SGU_CHUNK, SGU_CHUNK)
        b_blk = jnp.tile(b_sgu[l][:, :n_new], (1, reps)).T
        x = _merge(x, u, va_st, l, sb, w_blk, b_blk, g_a_r, g_b_r, w_out_b, l, tm_s)
        x = _mem_attn(x, ln_mem_r, w_mq_b, cmem_k, cmem_v, w_mo_b, l, tm_mem, n_new)
        x = _ffn(x, ln_ffn_r, w_g_b, w_u_b, w_d_b, ln_final_r, l, tm_s, l == depth - 1)
    y_sample = x.reshape(bs, n_new, d)

    return (y_prompt, y_sample, sb_k_prompt, sb_v_prompt,
            mk_p.reshape(depth, batch, n_mem, MEM_HEADS, d // MEM_HEADS),
            mv_p.reshape(depth, batch, n_mem, MEM_HEADS, d // MEM_HEADS),
            k_st.reshape(depth, bs, n_new, SB_HEADS, dh),
            v_st.reshape(depth, bs, n_new, SB_HEADS, dh),
            va_st.reshape(depth, bs, n_new, SGU_GROUPS, e))
```

```python
import functools

import jax
import jax.numpy as jnp
from jax import lax
from jax.experimental import pallas as pl
from jax.experimental.pallas import tpu as pltpu

F32 = jnp.float32
BF16 = jnp.bfloat16
EPS = 1e-6

LANES = 128
SGU_CHUNK = 128
SGU_GROUPS = 4
SB_HEADS = 8
MEM_HEADS = 4
ROW_TILE = 512
ROW_TILE_WIDE = 1024
SB_TILE = 256
SB_GROUP = 4
SB_SPLIT = 4
LOG2E = 1.4426950408889634
SB_EXIT_LOG = -106.0
VMEM_LIMIT = 56 * 1024 * 1024


def _params(n_axes):
    return pltpu.CompilerParams(dimension_semantics=("parallel",) * n_axes,
                                vmem_limit_bytes=VMEM_LIMIT)


def _resident(block_shape, index_map):
    return pl.BlockSpec(block_shape, index_map, pipeline_mode=pl.Buffered(1))


def _rmsnorm(x, g):
    return x * lax.rsqrt(jnp.mean(x * x, axis=-1, keepdims=True) + EPS) * g


def _log_keep(z):
    return -(jnp.maximum(z, 0.0) + jnp.log(1.0 + jnp.exp(-jnp.abs(z))))


def _cumsum_matrix():
    r = lax.broadcasted_iota(jnp.int32, (2 * LANES, 2 * LANES), 0) % LANES
    c = lax.broadcasted_iota(jnp.int32, (2 * LANES, 2 * LANES), 1)
    return jnp.where((c >= LANES) | (r >= c), 1.0, 0.0).astype(BF16)


def _block_suffix_sums(lk_blk, uo, carry):
    hi = lk_blk.astype(BF16)
    lo = (lk_blk - hi.astype(F32)).astype(BF16)
    r = jnp.dot(jnp.concatenate([hi, lo], axis=1), uo, preferred_element_type=F32)
    return r[:, :LANES] + carry, carry + r[:, LANES:]


def _in_proj_body(x_ref, ln_ref, w_ref, gv_ref, *refs, width, q_scale, n_stacks, kv_transposed):
    n_extra = 1 if kv_transposed else 0
    u_ref, va_ref, qs_ref, kb_ref, vb_ref, k_ref, v_ref = refs[n_extra + n_stacks:n_extra + n_stacks + 7]
    h = _rmsnorm(x_ref[...], ln_ref[...]).astype(BF16)

    def proj(j):
        return jnp.dot(h, w_ref[:, j * width:(j + 1) * width], preferred_element_type=F32)

    u_ref[...] = jax.nn.gelu(proj(0))
    va = jax.nn.gelu(proj(1))
    e = width // SGU_GROUPS
    for g in range(SGU_GROUPS):
        va_ref[:, g * e:(g + 1) * e] = _rmsnorm(va[:, g * e:(g + 1) * e], gv_ref[g:g + 1, :])
    qs_ref[...] = (proj(2) * q_scale).astype(BF16)
    if kv_transposed:
        kvt = lax.dot_general(refs[0][...], h, (((1,), (1,)), ((), ())), preferred_element_type=F32)
        k, v = kvt[:width], kvt[width:]
    else:
        k, v = proj(3), proj(4)
    k_ref[...] = k
    kb_ref[...] = k.astype(BF16)
    v_ref[...] = v
    vb_ref[...] = v.astype(BF16)


def _in_proj(x, ln, w_in, w_kvt, g_v, k_stack, v_stack, va_stack, layer, tm):
    n, d = x.shape
    width = d // 2
    in_w = w_in.shape[-1]
    row = lambda i: (i, 0)
    lay3 = lambda i: (layer, 0, 0)
    kv_transposed = w_kvt is not None
    f32_out = jax.ShapeDtypeStruct((n, width), F32)
    bf_out = jax.ShapeDtypeStruct((n, width), BF16)
    out_spec = pl.BlockSpec((tm, width), row)
    if kv_transposed:
        _, batch, _, t_len = k_stack.shape
        tpb = t_len // tm
        extra, extra_specs = [w_kvt], [_resident((None, 2 * width, d), lay3)]
        stacks = [k_stack, v_stack]
        stack_spec = pl.BlockSpec((None, None, width, tm), lambda i: (layer, i // tpb, 0, i % tpb))
        kb_shape = jax.ShapeDtypeStruct((batch, width, t_len), BF16)
        kb_spec = pl.BlockSpec((None, width, tm), lambda i: (i // tpb, 0, i % tpb))
        va_shape = jax.ShapeDtypeStruct((1, n, width), F32)
        va_spec = pl.BlockSpec((None, tm, width), lambda i: (0, i, 0))
        aliases = {5: 5, 6: 6}
    else:
        extra, extra_specs = [], []
        stacks = [k_stack, v_stack, va_stack]
        stack_spec = pl.BlockSpec((None, tm, width), lambda i: (layer, i, 0))
        kb_shape, kb_spec = bf_out, out_spec
        va_shape, va_spec = jax.ShapeDtypeStruct(va_stack.shape, F32), stack_spec
        aliases = {4: 5, 5: 6, 6: 1}
    return pl.pallas_call(
        functools.partial(_in_proj_body, width=width, q_scale=(width // SB_HEADS) ** -0.5, n_stacks=len(stacks),
                          kv_transposed=kv_transposed),
        grid=(n // tm,),
        in_specs=[pl.BlockSpec((tm, d), row),
                  _resident((None, 1, d), lay3),
                  _resident((None, d, in_w), lay3),
                  _resident((None, SGU_GROUPS, width // SGU_GROUPS), lay3)]
                 + extra_specs + [pl.BlockSpec(memory_space=pl.ANY)] * len(stacks),
        out_specs=[out_spec, va_spec, out_spec, kb_spec, kb_spec, stack_spec, stack_spec],
        out_shape=[f32_out, va_shape, bf_out, kb_shape, kb_shape,
                   jax.ShapeDtypeStruct(k_stack.shape, F32), jax.ShapeDtypeStruct(v_stack.shape, F32)],
        input_output_aliases=aliases,
        compiler_params=_params(1),
        name="in_proj",
    )(x, ln, w_in, g_v, *extra, *stacks)


def _neg_log_keep(z):
    return jnp.maximum(z, 0.0) + jnp.log(1.0 + jnp.exp2(jnp.abs(z) * (-LOG2E)))


def _neg_suffix_matrix(n):
    r = lax.broadcasted_iota(jnp.int32, (n, n), 0)
    c = lax.broadcasted_iota(jnp.int32, (n, n), 1)
    return jnp.where(r >= c, -1.0, 0.0).astype(BF16)


def _sb_prompt_body(q_ref, kt_ref, vt_ref, o_ref, nuo_ref, qm_ref, z_ref, l_ref, a_ref, carry_ref, acc_ref, *, dh):
    t_len, gw = q_ref.shape
    tile = SB_TILE
    n_heads = gw // dh
    rows = n_heads * tile
    part_rows = rows // SB_SPLIT
    slabs = part_rows // LANES
    nuo_ref[...] = _neg_suffix_matrix(tile)
    lane_head = lax.broadcasted_iota(jnp.int32, (tile, gw), 1) // dh

    def stage_logits(c, masked, slot, part):
        k0 = pl.multiple_of(c * tile, tile)
        base = part * part_rows
        return jnp.dot(qm_ref[base:base + part_rows, :], kt_ref[:, pl.ds(k0, tile)], preferred_element_type=F32)

    def stage_keep(c, masked, slot, part, z_all):
        base = part * part_rows
        for sl in range(slabs):
            r0 = base + sl * LANES
            z = z_all[sl * LANES:(sl + 1) * LANES]
            nlk = _neg_log_keep(z)
            if masked:
                qpos = (r0 % tile) + lax.broadcasted_iota(jnp.int32, (LANES, tile), 0)
                kpos = lax.broadcasted_iota(jnp.int32, (LANES, tile), 1)
                nlk = jnp.where(kpos < qpos, nlk, 0.0)
            z_ref[slot, r0:r0 + LANES, :] = z
            l_ref[slot, r0:r0 + LANES, :] = nlk.astype(BF16)
        return jnp.dot(l_ref[slot, base:base + part_rows, :], nuo_ref[...], preferred_element_type=F32)

    def stage_weights(c, masked, slot, part, r_all):
        k0 = pl.multiple_of(c * tile, tile)
        base = part * part_rows
        for sl in range(slabs):
            r0 = base + sl * LANES
            r = r_all[sl * LANES:(sl + 1) * LANES]
            carry = carry_ref[r0:r0 + LANES, :]
            cum = r + jnp.concatenate([carry] * (tile // LANES), axis=1)
            carry_ref[r0:r0 + LANES, :] = carry + jnp.broadcast_to(r[:, 0:1], (LANES, LANES))
            a = jnp.exp(z_ref[slot, r0:r0 + LANES, :] + cum)
            if masked:
                qpos = (r0 % tile) + lax.broadcasted_iota(jnp.int32, (LANES, tile), 0)
                kpos = lax.broadcasted_iota(jnp.int32, (LANES, tile), 1)
                a = jnp.where(kpos < qpos, a, 0.0)
            a_ref[slot, r0:r0 + LANES, :] = a.astype(BF16)
        acc_ref[base:base + part_rows, :] += lax.dot_general(
            a_ref[slot, base:base + part_rows, :], vt_ref[:, pl.ds(k0, tile)], (((1,), (1,)), ((), ())),
            preferred_element_type=F32)

    def fold(*tiles):
        items = [(c, masked, slot, part) for (c, masked, slot) in tiles for part in range(SB_SPLIT)]
        logits, sums = {}, {}
        for step in range(len(items) + 2):
            if step < len(items):
                logits[step] = stage_logits(*items[step])
            if 0 <= step - 1 < len(items):
                sums[step - 1] = stage_keep(*items[step - 1], logits.pop(step - 1))
            if 0 <= step - 2 < len(items):
                stage_weights(*items[step - 2], sums.pop(step - 2))

    def begin(q0):
        q = q_ref[pl.ds(q0, tile), :].astype(F32)
        for h in range(n_heads):
            qm_ref[h * tile:(h + 1) * tile, :] = jnp.where(lane_head == h, q, 0.0).astype(BF16)
        carry_ref[...] = jnp.zeros_like(carry_ref)
        acc_ref[...] = jnp.zeros_like(acc_ref)

    def finish(q0):
        out = acc_ref[0:tile, :]
        for h in range(1, n_heads):
            out = jnp.where(lane_head == h, acc_ref[h * tile:(h + 1) * tile, :], out)
        o_ref[pl.ds(q0, tile), :] = out

    begin(0)
    fold((0, True, 0))
    finish(0)

    def q_tile(j, _):
        q0 = pl.multiple_of(j * tile, tile)
        begin(q0)
        fold((j, True, 0), (j - 1, False, 1))

        def more(state):
            c, max_carry = state
            return (c >= 0) & (max_carry >= SB_EXIT_LOG)

        def earlier(state):
            c, _ = state
            fold((c, False, 0))
            return c - 1, jnp.max(carry_ref[...])

        lax.while_loop(more, earlier, (j - 2, jnp.max(carry_ref[...])))
        finish(q0)
        return 0

    lax.fori_loop(1, t_len // tile, q_tile, 0)


def _sb_prompt(qs, ktb, vtb, batch, t_len):
    n, width = qs.shape
    dh = width // SB_HEADS
    gw = SB_GROUP * dh
    groups = width // gw
    rows = SB_GROUP * SB_TILE
    blk = pl.BlockSpec((t_len, gw), lambda b, g: (b, g))
    blk_t = pl.BlockSpec((None, gw, t_len), lambda b, g: (b, g, 0))
    return pl.pallas_call(
        functools.partial(_sb_prompt_body, dh=dh),
        grid=(batch, groups),
        in_specs=[blk, blk_t, blk_t],
        out_specs=blk,
        out_shape=jax.ShapeDtypeStruct((n, width), F32),
        scratch_shapes=[pltpu.VMEM((SB_TILE, SB_TILE), BF16),
                        pltpu.VMEM((rows, gw), BF16),
                        pltpu.VMEM((2, rows, SB_TILE), F32),
                        pltpu.VMEM((2, rows, SB_TILE), BF16),
                        pltpu.VMEM((2, rows, SB_TILE), BF16),
                        pltpu.VMEM((rows, LANES), F32),
                        pltpu.VMEM((rows, gw), F32)],
        compiler_params=_params(2),
        name="sb_prompt",
    )(qs, ktb, vtb)


def _sb_sample_body(q_ref, kc_ref, vc_ref, kn_ref, vn_ref, o_ref, kt_all, vt_all, *, past, n_new):
    width = q_ref.shape[1]
    dh = width // SB_HEADS
    rows = SB_HEADS * n_new
    n_keys = kt_all.shape[1]
    zeros = jnp.zeros((width, n_keys - past - n_new), F32)
    kt_all[:, 0:past] = kc_ref[...].astype(BF16)
    vt_all[:, 0:past] = vc_ref[...].astype(BF16)
    kt_all[:, past:] = jnp.concatenate([kn_ref[...].astype(F32).T, zeros], axis=1).astype(BF16)
    vt_all[:, past:] = jnp.concatenate([vn_ref[...].astype(F32).T, zeros], axis=1).astype(BF16)

    q_rep = jnp.concatenate([q_ref[...].astype(F32)] * SB_HEADS, axis=0)
    row_head = lax.broadcasted_iota(jnp.int32, (rows, width), 0) // n_new
    col_head = lax.broadcasted_iota(jnp.int32, (rows, width), 1) // dh
    own = row_head == col_head
    q_bd = jnp.where(own, q_rep, 0.0).astype(BF16)

    z = jnp.dot(q_bd, kt_all[...], preferred_element_type=F32)
    kpos = lax.broadcasted_iota(jnp.int32, (rows, n_keys), 1)
    qpos = past + lax.broadcasted_iota(jnp.int32, (rows, n_keys), 0) % n_new
    valid = kpos < qpos
    lk = jnp.where(valid, _log_keep(z), 0.0)
    uo = _cumsum_matrix()
    n_blk = n_keys // LANES
    carry = jnp.zeros((rows, LANES), F32)
    cums = [None] * n_blk
    for s in reversed(range(n_blk)):
        cums[s], carry = _block_suffix_sums(lk[:, s * LANES:(s + 1) * LANES], uo, carry)
    a = jnp.where(valid, jnp.exp(z + jnp.concatenate(cums, axis=1)), 0.0)
    o_all = lax.dot_general(a.astype(BF16), vt_all[...], (((1,), (1,)), ((), ())),
                            preferred_element_type=F32)
    o_all = jnp.where(own, o_all, 0.0)
    out = o_all[0:n_new, :]
    for h in range(1, SB_HEADS):
        out = out + o_all[h * n_new:(h + 1) * n_new, :]
    o_ref[...] = out


def _sb_sample(qs, kb, vb, cache_kt, cache_vt, layer, n_new):
    n, width = qs.shape
    past = cache_kt.shape[3]
    assert past % LANES == 0 and n_new % 8 == 0 and n_new <= LANES
    n_keys = past + LANES
    row = lambda b: (b, 0)
    cache = lambda b: (layer, b, 0, 0)
    return pl.pallas_call(
        functools.partial(_sb_sample_body, past=past, n_new=n_new),
        grid=(n // n_new,),
        in_specs=[pl.BlockSpec((n_new, width), row),
                  pl.BlockSpec((None, None, width, past), cache),
                  pl.BlockSpec((None, None, width, past), cache),
                  pl.BlockSpec((n_new, width), row),
                  pl.BlockSpec((n_new, width), row)],
        out_specs=pl.BlockSpec((n_new, width), row),
        out_shape=jax.ShapeDtypeStruct((n, width), F32),
        scratch_shapes=[pltpu.VMEM((width, n_keys), BF16), pltpu.VMEM((width, n_keys), BF16)],
        compiler_params=_params(1),
        name="sb_sample",
    )(qs, cache_kt, cache_vt, kb, vb)


def _merge_body(x_ref, u_ref, va_ref, sb_ref, ws_ref, bs_ref, ga_ref, gb_ref, wo_ref, o_ref, cat_ref):
    tm = x_ref.shape[0]
    width = u_ref.shape[1]
    e = width // SGU_GROUPS
    r = lax.broadcasted_iota(jnp.int32, (SGU_CHUNK, SGU_CHUNK), 0)
    c = lax.broadcasted_iota(jnp.int32, (SGU_CHUNK, SGU_CHUNK), 1)
    n_chunks = tm // SGU_CHUNK
    cat_ref[:, width:] = _rmsnorm(sb_ref[...], gb_ref[...]).astype(BF16)
    va = va_ref[...].astype(BF16)
    mixes = []
    for g in range(SGU_GROUPS):
        w_g = jnp.where(r >= c, ws_ref[g], 0.0).astype(BF16)
        chunks = jnp.concatenate([va[ch * SGU_CHUNK:(ch + 1) * SGU_CHUNK, g * e:(g + 1) * e]
                                  for ch in range(n_chunks)], axis=1)
        mixes.append(jnp.dot(w_g, chunks, preferred_element_type=F32) + bs_ref[:, g:g + 1])
    mix = jnp.concatenate([jnp.concatenate([m[:, ch * e:(ch + 1) * e] for m in mixes], axis=1)
                           for ch in range(n_chunks)], axis=0)
    cat_ref[:, 0:width] = _rmsnorm(u_ref[...] * mix, ga_ref[...]).astype(BF16)
    o_ref[...] = x_ref[...] + jnp.dot(cat_ref[...], wo_ref[...], preferred_element_type=F32)


def _merge(x, u, va, va_layer, sb, w_mix, b_mix, g_a, g_b, w_out, layer, tm):
    n, d = x.shape
    width = u.shape[1]
    sb_width = sb.shape[1]
    row = lambda i: (i, 0)
    fix2 = lambda i: (0, 0)
    fix3 = lambda i: (0, 0, 0)
    lay3 = lambda i: (layer, 0, 0)
    return pl.pallas_call(
        _merge_body,
        grid=(n // tm,),
        in_specs=[pl.BlockSpec((tm, d), row),
                  pl.BlockSpec((tm, width), row),
                  pl.BlockSpec((None, tm, width), lambda i: (va_layer, i, 0)),
                  pl.BlockSpec((tm, sb_width), row),
                  _resident((SGU_GROUPS, SGU_CHUNK, SGU_CHUNK), fix3),
                  _resident((SGU_CHUNK, SGU_GROUPS), fix2),
                  _resident((None, 1, width), lay3),
                  _resident((None, 1, sb_width), lay3),
                  _resident((None, width + sb_width, d), lay3)],
        out_specs=pl.BlockSpec((tm, d), row),
        out_shape=jax.ShapeDtypeStruct((n, d), F32),
        scratch_shapes=[pltpu.VMEM((tm, width + sb_width), BF16)],
        compiler_params=_params(1),
        name="merge",
    )(x, u, va, sb, w_mix, b_mix, g_a, g_b, w_out)


def _mem_kv_body(m_ref, wk_ref, wv_ref, k_ref, v_ref):
    m = m_ref[...].astype(BF16)
    k_ref[...] = jnp.dot(m, wk_ref[...], preferred_element_type=F32)
    v_ref[...] = jnp.dot(m, wv_ref[...], preferred_element_type=F32)


def _mem_kv(mem, w_mk, w_mv):
    b, m, d = mem.shape
    depth = w_mk.shape[0]
    out = jax.ShapeDtypeStruct((depth, b, m, d), F32)
    w_spec = pl.BlockSpec((None, d, d), lambda l, i: (l, 0, 0))
    o_spec = pl.BlockSpec((None, None, m, d), lambda l, i: (l, i, 0, 0))
    return pl.pallas_call(
        _mem_kv_body,
        grid=(depth, b),
        in_specs=[pl.BlockSpec((None, m, d), lambda l, i: (i, 0, 0)), w_spec, w_spec],
        out_specs=[o_spec, o_spec],
        out_shape=[out, out],
        compiler_params=_params(2),
        name="mem_kv",
    )(mem, w_mk, w_mv)


def _mem_attn_body(x_ref, ln_ref, wq_ref, mk_ref, mv_ref, wo_ref, o_ref, oc_ref, *, n_streams):
    tm, d = x_ref.shape
    dh = d // MEM_HEADS
    rpb = tm // n_streams
    x = x_ref[...]
    h = _rmsnorm(x, ln_ref[...]).astype(BF16)
    q = (jnp.dot(h, wq_ref[...], preferred_element_type=F32) * dh ** -0.5)
    nt_dims = (((1,), (1,)), ((), ()))
    if MEM_HEADS * rpb <= LANES:
        stacked = MEM_HEADS * rpb
        own = (lax.broadcasted_iota(jnp.int32, (stacked, d), 0) // rpb
               == lax.broadcasted_iota(jnp.int32, (stacked, d), 1) // dh)
        for s in range(n_streams):
            rows = slice(s * rpb, (s + 1) * rpb)
            q_bd = jnp.where(own, jnp.concatenate([q[rows]] * MEM_HEADS, axis=0), 0.0).astype(BF16)
            k_s = jnp.concatenate([mk_ref[s, hd] for hd in range(MEM_HEADS)], axis=1).astype(BF16)
            v_s = jnp.concatenate([mv_ref[s, hd] for hd in range(MEM_HEADS)], axis=1).astype(BF16)
            sc = lax.dot_general(q_bd, k_s, nt_dims, preferred_element_type=F32)
            p = jnp.exp(sc - jnp.max(sc, axis=-1, keepdims=True))
            o = jnp.dot(p.astype(BF16), v_s, preferred_element_type=F32)
            o = jnp.where(own, o / jnp.sum(p, axis=-1, keepdims=True), 0.0)
            o_s = o[0:rpb]
            for hd in range(1, MEM_HEADS):
                o_s = o_s + o[hd * rpb:(hd + 1) * rpb]
            oc_ref[rows, :] = o_s.astype(BF16)
        o_ref[...] = x + jnp.dot(oc_ref[...], wo_ref[...], preferred_element_type=F32)
        return
    q = q.astype(BF16)
    for s in range(n_streams):
        rows = slice(s * rpb, (s + 1) * rpb)
        for hd in range(MEM_HEADS):
            cols = slice(hd * dh, (hd + 1) * dh)
            k = mk_ref[s, :, cols].astype(BF16)
            v = mv_ref[s, :, cols].astype(BF16)
            sc = lax.dot_general(q[rows, cols], k, (((1,), (1,)), ((), ())), preferred_element_type=F32)
            p = jnp.exp(sc - jnp.max(sc, axis=-1, keepdims=True))
            o = jnp.dot(p.astype(BF16), v, preferred_element_type=F32) / jnp.sum(p, axis=-1, keepdims=True)
            oc_ref[rows, cols] = o.astype(BF16)
    o_ref[...] = x + jnp.dot(oc_ref[...], wo_ref[...], preferred_element_type=F32)


def _mem_attn(x, ln, w_mq, mk, mv, w_mo, layer, tm, rows_per_stream):
    n, d = x.shape
    m = mk.shape[-2] if mk.ndim == 5 else mk.shape[2]
    n_streams = max(1, tm // rows_per_stream)
    tiles_per_stream = max(1, rows_per_stream // tm)
    row = lambda i: (i, 0)
    lay3 = lambda i: (layer, 0, 0)
    mem = lambda i: (layer, i // tiles_per_stream, 0, 0)
    mem_spec = pl.BlockSpec((None, n_streams, m, d), mem)
    if mk.ndim == 5:
        mem_spec = pl.BlockSpec((None, n_streams) + mk.shape[2:], lambda i: (layer, i // tiles_per_stream, 0, 0, 0))
    return pl.pallas_call(
        functools.partial(_mem_attn_body, n_streams=n_streams),
        grid=(n // tm,),
        in_specs=[pl.BlockSpec((tm, d), row),
                  _resident((None, 1, d), lay3),
                  _resident((None, d, d), lay3),
                  mem_spec,
                  mem_spec,
                  _resident((None, d, d), lay3)],
        out_specs=pl.BlockSpec((tm, d), row),
        out_shape=jax.ShapeDtypeStruct((n, d), F32),
        scratch_shapes=[pltpu.VMEM((tm, d), BF16)],
        compiler_params=_params(1),
        name="mem_attn",
    )(x, ln, w_mq, mk, mv, w_mo)


def _ffn_body(x_ref, ln_ref, wg_ref, wu_ref, wd_ref, lnf_ref, o_ref, *, final_norm):
    x = x_ref[...]
    h = _rmsnorm(x, ln_ref[...]).astype(BF16)
    gate = jnp.dot(h, wg_ref[...], preferred_element_type=F32)
    up = jnp.dot(h, wu_ref[...], preferred_element_type=F32)
    act = (jax.nn.silu(gate) * up).astype(BF16)
    y = x + jnp.dot(act, wd_ref[...], preferred_element_type=F32)
    if final_norm:
        y = _rmsnorm(y, lnf_ref[...])
    o_ref[...] = y


def _ffn(x, ln, w_gate, w_up, w_down, ln_final, layer, tm, final_norm):
    n, d = x.shape
    d_ff = w_gate.shape[-1]
    row = lambda i: (i, 0)
    lay3 = lambda i: (layer, 0, 0)
    return pl.pallas_call(
        functools.partial(_ffn_body, final_norm=final_norm),
        grid=(n // tm,),
        in_specs=[pl.BlockSpec((tm, d), row),
                  _resident((None, 1, d), lay3),
                  _resident((None, d, d_ff), lay3),
                  _resident((None, d, d_ff), lay3),
                  _resident((None, d_ff, d), lay3),
                  _resident((1, d), lambda i: (0, 0))],
        out_specs=pl.BlockSpec((tm, d), row),
        out_shape=jax.ShapeDtypeStruct((n, d), F32),
        compiler_params=_params(1),
        name="ffn",
    )(x, ln, w_gate, w_up, w_down, ln_final)


def kernel(x_prompt, x_sample, cache_sb_k, cache_sb_v, cache_mem_k, cache_mem_v, mem_prompt, ln_mix, w_in, g_sgu_v, w_sgu, b_sgu, g_out_sgu, g_out_sb, w_out, ln_mem, w_mq, w_mk, w_mv, w_mo, ln_ffn, w_ffn_gate, w_ffn_up, w_ffn_down, ln_final):
    batch, t_len, d = x_prompt.shape
    bs, n_new, _ = x_sample.shape
    depth = w_in.shape[0]
    past = cache_sb_k.shape[2]
    n_mem = mem_prompt.shape[1]
    width = d // 2
    dh = width // SB_HEADS
    e = width // SGU_GROUPS
    assert t_len % ROW_TILE == 0 and t_len % SB_TILE == 0 and SGU_CHUNK % n_new == 0

    w_in_b, w_out_b = w_in.astype(BF16), w_out.astype(BF16)
    w_kvt_b = jnp.swapaxes(w_in_b[:, :, 3 * width:], 1, 2)
    w_mq_b, w_mk_b, w_mv_b, w_mo_b = (w.astype(BF16) for w in (w_mq, w_mk, w_mv, w_mo))
    w_g_b, w_u_b, w_d_b = (w.astype(BF16) for w in (w_ffn_gate, w_ffn_up, w_ffn_down))
    as_rows = lambda g: g.reshape(depth, 1, -1)
    ln_mix_r, ln_mem_r, ln_ffn_r = as_rows(ln_mix), as_rows(ln_mem), as_rows(ln_ffn)
    g_a_r, g_b_r = as_rows(g_out_sgu), as_rows(g_out_sb)
    ln_final_r = ln_final.reshape(1, d)

    cache_kt = cache_sb_k.transpose(0, 1, 3, 4, 2).reshape(depth, bs, width, past)
    cache_vt = cache_sb_v.transpose(0, 1, 3, 4, 2).reshape(depth, bs, width, past)
    cmem_k = cache_mem_k.transpose(0, 1, 3, 2, 4)
    cmem_v = cache_mem_v.transpose(0, 1, 3, 2, 4)

    mk_p, mv_p = _mem_kv(mem_prompt, w_mk_b, w_mv_b)

    n_p = batch * t_len
    x = x_prompt.reshape(n_p, d)
    kt_st = jnp.zeros((depth, batch, width, t_len), F32)
    vt_st = jnp.zeros((depth, batch, width, t_len), F32)
    for l in range(depth):
        u, va, qs, ktb, vtb, kt_st, vt_st = _in_proj(x, ln_mix_r, w_in_b, w_kvt_b, g_sgu_v, kt_st, vt_st, None, l, ROW_TILE_WIDE)
        sb = _sb_prompt(qs, ktb, vtb, batch, t_len)
        x = _merge(x, u, va, 0, sb, w_sgu[l], b_sgu[l].T, g_a_r, g_b_r, w_out_b, l, ROW_TILE)
        x = _mem_attn(x, ln_mem_r, w_mq_b, mk_p, mv_p, w_mo_b, l, ROW_TILE_WIDE, t_len)
        x = _ffn(x, ln_ffn_r, w_g_b, w_u_b, w_d_b, ln_final_r, l, ROW_TILE, l == depth - 1)
    y_prompt = x.reshape(batch, t_len, d)
    sb_k_prompt = kt_st.reshape(depth, batch, SB_HEADS, dh, t_len).transpose(0, 1, 4, 2, 3)
    sb_v_prompt = vt_st.reshape(depth, batch, SB_HEADS, dh, t_len).transpose(0, 1, 4, 2, 3)

    n_s = bs * n_new
    tm_s = min(ROW_TILE, n_s)
    tm_mem = 4 * n_new
    reps = SGU_CHUNK // n_new
    x = x_sample.reshape(n_s, d)
    k_st = jnp.zeros((depth, n_s, width), F32)
    v_st = jnp.zeros((depth, n_s, width), F32)
    va_st = jnp.zeros((depth, n_s, width), F32)
    for l in range(depth):
        u, va_st, qs, kb, vb, k_st, v_st = _in_proj(x, ln_mix_r, w_in_b, None, g_sgu_v, k_st, v_st, va_st, l, tm_s)
        sb = _sb_sample(qs, kb, vb, cache_kt, cache_vt, l, n_new)
        w_blk = jnp.einsum('ab,gts->gatbs', jnp.eye(reps, dtype=F32), w_sgu[l][:, :n_new, :n_new])
        w_blk = w_blk.reshape(SGU_GROUPS, SGU_CHUNK, SGU_CHUNK)
        b_blk = jnp.tile(b_sgu[l][:, :n_new], (1, reps)).T
        x = _merge(x, u, va_st, l, sb, w_blk, b_blk, g_a_r, g_b_r, w_out_b, l, tm_s)
        x = _mem_attn(x, ln_mem_r, w_mq_b, cmem_k, cmem_v, w_mo_b, l, tm_mem, n_new)
        x = _ffn(x, ln_ffn_r, w_g_b, w_u_b, w_d_b, ln_final_r, l, tm_s, l == depth - 1)
    y_sample = x.reshape(bs, n_new, d)

    return (y_prompt, y_sample, sb_k_prompt, sb_v_prompt,
            mk_p.reshape(depth, batch, n_mem, MEM_HEADS, d // MEM_HEADS),
            mv_p.reshape(depth, batch, n_mem, MEM_HEADS, d // MEM_HEADS),
            k_st.reshape(depth, bs, n_new, SB_HEADS, dh),
            v_st.reshape(depth, bs, n_new, SB_HEADS, dh),
            va_st.reshape(depth, bs, n_new, SGU_GROUPS, e))
```

```python
import functools

import jax
import jax.numpy as jnp
from jax import lax
from jax.experimental import pallas as pl
from jax.experimental.pallas import tpu as pltpu

F32 = jnp.float32
BF16 = jnp.bfloat16
EPS = 1e-6

LANES = 128
SGU_CHUNK = 128
SGU_GROUPS = 4
SB_HEADS = 8
MEM_HEADS = 4
ROW_TILE = 512
ROW_TILE_WIDE = 1024
SB_TILE = 256
SB_GROUP = 4
SB_SPLIT = 4
LOG2E = 1.4426950408889634
SB_EXIT_LOG = -106.0
VMEM_LIMIT = 56 * 1024 * 1024


def _params(n_axes):
    return pltpu.CompilerParams(dimension_semantics=("parallel",) * n_axes,
                                vmem_limit_bytes=VMEM_LIMIT)


def _resident(block_shape, index_map):
    return pl.BlockSpec(block_shape, index_map, pipeline_mode=pl.Buffered(1))


def _rmsnorm(x, g):
    return x * lax.rsqrt(jnp.mean(x * x, axis=-1, keepdims=True) + EPS) * g


def _in_proj_body(x_ref, ln_ref, w_ref, gv_ref, *refs, width, q_scale, n_stacks, kv_transposed):
    n_extra = 1 if kv_transposed else 0
    u_ref, va_ref, qs_ref, kb_ref, vb_ref, k_ref, v_ref = refs[n_extra + n_stacks:n_extra + n_stacks + 7]
    h = _rmsnorm(x_ref[...], ln_ref[...]).astype(BF16)

    def proj(j):
        return jnp.dot(h, w_ref[:, j * width:(j + 1) * width], preferred_element_type=F32)

    u_ref[...] = jax.nn.gelu(proj(0))
    va = jax.nn.gelu(proj(1))
    e = width // SGU_GROUPS
    for g in range(SGU_GROUPS):
        va_ref[:, g * e:(g + 1) * e] = _rmsnorm(va[:, g * e:(g + 1) * e], gv_ref[g:g + 1, :])
    qs_ref[...] = (proj(2) * q_scale).astype(BF16)
    if kv_transposed:
        kvt = lax.dot_general(refs[0][...], h, (((1,), (1,)), ((), ())), preferred_element_type=F32)
        k, v = kvt[:width], kvt[width:]
    else:
        k, v = proj(3), proj(4)
    k_ref[...] = k
    kb_ref[...] = k.astype(BF16)
    v_ref[...] = v
    vb_ref[...] = v.astype(BF16)


def _in_proj(x, ln, w_in, w_kvt, g_v, k_stack, v_stack, va_stack, layer, tm):
    n, d = x.shape
    width = d // 2
    in_w = w_in.shape[-1]
    row = lambda i: (i, 0)
    lay3 = lambda i: (layer, 0, 0)
    kv_transposed = w_kvt is not None
    f32_out = jax.ShapeDtypeStruct((n, width), F32)
    bf_out = jax.ShapeDtypeStruct((n, width), BF16)
    out_spec = pl.BlockSpec((tm, width), row)
    if kv_transposed:
        _, batch, _, t_len = k_stack.shape
        tpb = t_len // tm
        extra, extra_specs = [w_kvt], [_resident((None, 2 * width, d), lay3)]
        stacks = [k_stack, v_stack]
        stack_spec = pl.BlockSpec((None, None, width, tm), lambda i: (layer, i // tpb, 0, i % tpb))
        kb_shape = jax.ShapeDtypeStruct((batch, width, t_len), BF16)
        kb_spec = pl.BlockSpec((None, width, tm), lambda i: (i // tpb, 0, i % tpb))
        va_shape = jax.ShapeDtypeStruct((1, n, width), F32)
        va_spec = pl.BlockSpec((None, tm, width), lambda i: (0, i, 0))
        aliases = {5: 5, 6: 6}
    else:
        extra, extra_specs = [], []
        stacks = [k_stack, v_stack, va_stack]
        stack_spec = pl.BlockSpec((None, tm, width), lambda i: (layer, i, 0))
        kb_shape, kb_spec = bf_out, out_spec
        va_shape, va_spec = jax.ShapeDtypeStruct(va_stack.shape, F32), stack_spec
        aliases = {4: 5, 5: 6, 6: 1}
    return pl.pallas_call(
        functools.partial(_in_proj_body, width=width, q_scale=(width // SB_HEADS) ** -0.5, n_stacks=len(stacks),
                          kv_transposed=kv_transposed),
        grid=(n // tm,),
        in_specs=[pl.BlockSpec((tm, d), row),
                  _resident((None, 1, d), lay3),
                  _resident((None, d, in_w), lay3),
                  _resident((None, SGU_GROUPS, width // SGU_GROUPS), lay3)]
                 + extra_specs + [pl.BlockSpec(memory_space=pl.ANY)] * len(stacks),
        out_specs=[out_spec, va_spec, out_spec, kb_spec, kb_spec, stack_spec, stack_spec],
        out_shape=[f32_out, va_shape, bf_out, kb_shape, kb_shape,
                   jax.ShapeDtypeStruct(k_stack.shape, F32), jax.ShapeDtypeStruct(v_stack.shape, F32)],
        input_output_aliases=aliases,
        compiler_params=_params(1),
        name="in_proj",
    )(x, ln, w_in, g_v, *extra, *stacks)


def _neg_log_keep(z):
    return jnp.maximum(z, 0.0) + jnp.log(1.0 + jnp.exp2(jnp.abs(z) * (-LOG2E)))


def _neg_suffix_matrix(n):
    r = lax.broadcasted_iota(jnp.int32, (n, n), 0)
    c = lax.broadcasted_iota(jnp.int32, (n, n), 1)
    return jnp.where(r >= c, -1.0, 0.0).astype(BF16)


def _sb_prompt_body(q_ref, kt_ref, vt_ref, o_ref, nuo_ref, qm_ref, z_ref, l_ref, a_ref, carry_ref, acc_ref, *, dh):
    t_len, gw = q_ref.shape
    tile = SB_TILE
    n_heads = gw // dh
    rows = n_heads * tile
    part_rows = rows // SB_SPLIT
    slabs = part_rows // LANES
    nuo_ref[...] = _neg_suffix_matrix(tile)
    lane_head = lax.broadcasted_iota(jnp.int32, (tile, gw), 1) // dh

    def stage_logits(c, masked, slot, part):
        k0 = pl.multiple_of(c * tile, tile)
        base = part * part_rows
        return jnp.dot(qm_ref[base:base + part_rows, :], kt_ref[:, pl.ds(k0, tile)], preferred_element_type=F32)

    def stage_keep(c, masked, slot, part, z_all):
        base = part * part_rows
        for sl in range(slabs):
            r0 = base + sl * LANES
            z = z_all[sl * LANES:(sl + 1) * LANES]
            nlk = _neg_log_keep(z)
            if masked:
                qpos = (r0 % tile) + lax.broadcasted_iota(jnp.int32, (LANES, tile), 0)
                kpos = lax.broadcasted_iota(jnp.int32, (LANES, tile), 1)
                nlk = jnp.where(kpos < qpos, nlk, 0.0)
            z_ref[slot, r0:r0 + LANES, :] = z
            l_ref[slot, r0:r0 + LANES, :] = nlk.astype(BF16)
        return jnp.dot(l_ref[slot, base:base + part_rows, :], nuo_ref[...], preferred_element_type=F32)

    def stage_weights(c, masked, slot, part, r_all):
        k0 = pl.multiple_of(c * tile, tile)
        base = part * part_rows
        for sl in range(slabs):
            r0 = base + sl * LANES
            r = r_all[sl * LANES:(sl + 1) * LANES]
            carry = carry_ref[r0:r0 + LANES, :]
            cum = r + jnp.concatenate([carry] * (tile // LANES), axis=1)
            carry_ref[r0:r0 + LANES, :] = carry + jnp.broadcast_to(r[:, 0:1], (LANES, LANES))
            a = jnp.exp(z_ref[slot, r0:r0 + LANES, :] + cum)
            if masked:
                qpos = (r0 % tile) + lax.broadcasted_iota(jnp.int32, (LANES, tile), 0)
                kpos = lax.broadcasted_iota(jnp.int32, (LANES, tile), 1)
                a = jnp.where(kpos < qpos, a, 0.0)
            a_ref[slot, r0:r0 + LANES, :] = a.astype(BF16)
        acc_ref[base:base + part_rows, :] += lax.dot_general(
            a_ref[slot, base:base + part_rows, :], vt_ref[:, pl.ds(k0, tile)], (((1,), (1,)), ((), ())),
            preferred_element_type=F32)

    def fold(*tiles):
        items = [(c, masked, slot, part) for (c, masked, slot) in tiles for part in range(SB_SPLIT)]
        logits, sums = {}, {}
        for step in range(len(items) + 2):
            if step < len(items):
                logits[step] = stage_logits(*items[step])
            if 0 <= step - 1 < len(items):
                sums[step - 1] = stage_keep(*items[step - 1], logits.pop(step - 1))
            if 0 <= step - 2 < len(items):
                stage_weights(*items[step - 2], sums.pop(step - 2))

    def begin(q0):
        q = q_ref[pl.ds(q0, tile), :].astype(F32)
        for h in range(n_heads):
            qm_ref[h * tile:(h + 1) * tile, :] = jnp.where(lane_head == h, q, 0.0).astype(BF16)
        carry_ref[...] = jnp.zeros_like(carry_ref)
        acc_ref[...] = jnp.zeros_like(acc_ref)

    def finish(q0):
        out = acc_ref[0:tile, :]
        for h in range(1, n_heads):
            out = jnp.where(lane_head == h, acc_ref[h * tile:(h + 1) * tile, :], out)
        o_ref[pl.ds(q0, tile), :] = out

    begin(0)
    fold((0, True, 0))
    finish(0)

    def q_tile(j, _):
        q0 = pl.multiple_of(j * tile, tile)
        begin(q0)
        fold((j, True, 0), (j - 1, False, 1))

        def more(state):
            c, max_carry = state
            return (c >= 0) & (max_carry >= SB_EXIT_LOG)

        def earlier(state):
            c, _ = state
            fold((c, False, 0))
            return c - 1, jnp.max(carry_ref[...])

        lax.while_loop(more, earlier, (j - 2, jnp.max(carry_ref[...])))
        finish(q0)
        return 0

    lax.fori_loop(1, t_len // tile, q_tile, 0)


def _sb_prompt(qs, ktb, vtb, batch, t_len):
    n, width = qs.shape
    dh = width // SB_HEADS
    gw = SB_GROUP * dh
    groups = width // gw
    rows = SB_GROUP * SB_TILE
    blk = pl.BlockSpec((t_len, gw), lambda b, g: (b, g))
    blk_t = pl.BlockSpec((None, gw, t_len), lambda b, g: (b, g, 0))
    return pl.pallas_call(
        functools.partial(_sb_prompt_body, dh=dh),
        grid=(batch, groups),
        in_specs=[blk, blk_t, blk_t],
        out_specs=blk,
        out_shape=jax.ShapeDtypeStruct((n, width), F32),
        scratch_shapes=[pltpu.VMEM((SB_TILE, SB_TILE), BF16),
                        pltpu.VMEM((rows, gw), BF16),
                        pltpu.VMEM((2, rows, SB_TILE), F32),
                        pltpu.VMEM((2, rows, SB_TILE), BF16),
                        pltpu.VMEM((2, rows, SB_TILE), BF16),
                        pltpu.VMEM((rows, LANES), F32),
                        pltpu.VMEM((rows, gw), F32)],
        compiler_params=_params(2),
        name="sb_prompt",
    )(qs, ktb, vtb)


def _sb_sample_body(q_ref, kc_ref, vc_ref, kn_ref, vn_ref, o_ref, nsm_ref, carry_ref, acc_ref, *, past, n_new):
    width = q_ref.shape[1]
    dh = width // SB_HEADS
    rows = SB_HEADS * n_new
    tile = SB_TILE
    nsm_ref[...] = _neg_suffix_matrix(tile)

    q_rep = jnp.concatenate([q_ref[...].astype(F32)] * SB_HEADS, axis=0)
    own = (lax.broadcasted_iota(jnp.int32, (rows, width), 0) // n_new
           == lax.broadcasted_iota(jnp.int32, (rows, width), 1) // dh)
    q_bd = jnp.where(own, q_rep, 0.0).astype(BF16)

    def fold(kt_blk, vt_blk, valid):
        nk = kt_blk.shape[1]
        z = jnp.dot(q_bd, kt_blk, preferred_element_type=F32)
        nlk = _neg_log_keep(z)
        if valid is not None:
            nlk = jnp.where(valid, nlk, 0.0)
        r = jnp.dot(nlk.astype(BF16), nsm_ref[0:nk, 0:nk], preferred_element_type=F32)
        carry = carry_ref[...]
        a = jnp.exp(z + r + jnp.concatenate([carry] * (nk // LANES), axis=1))
        if valid is not None:
            a = jnp.where(valid, a, 0.0)
        carry_ref[...] = carry + jnp.broadcast_to(r[:, 0:1], (rows, LANES))
        acc_ref[...] += lax.dot_general(a.astype(BF16), vt_blk, (((1,), (1,)), ((), ())),
                                        preferred_element_type=F32)

    zeros = jnp.zeros((width, LANES - n_new), F32)
    kt_new = jnp.concatenate([kn_ref[...].astype(F32).T, zeros], axis=1).astype(BF16)
    vt_new = jnp.concatenate([vn_ref[...].astype(F32).T, zeros], axis=1).astype(BF16)
    valid_new = (lax.broadcasted_iota(jnp.int32, (rows, LANES), 1)
                 < lax.broadcasted_iota(jnp.int32, (rows, LANES), 0) % n_new)
    carry_ref[...] = jnp.zeros_like(carry_ref)
    acc_ref[...] = jnp.zeros_like(acc_ref)
    fold(kt_new, vt_new, valid_new)

    def more(state):
        c, max_carry = state
        return (c >= 0) & (max_carry >= SB_EXIT_LOG)

    def earlier(state):
        c, _ = state
        k0 = pl.multiple_of(c * tile, tile)
        fold(kc_ref[:, pl.ds(k0, tile)].astype(BF16), vc_ref[:, pl.ds(k0, tile)].astype(BF16), None)
        return c - 1, jnp.max(carry_ref[...])

    lax.while_loop(more, earlier, (past // tile - 1, jnp.max(carry_ref[...])))
    o_all = jnp.where(own, acc_ref[...], 0.0)
    out = o_all[0:n_new, :]
    for h in range(1, SB_HEADS):
        out = out + o_all[h * n_new:(h + 1) * n_new, :]
    o_ref[...] = out


def _sb_sample(qs, kb, vb, cache_kt, cache_vt, layer, n_new):
    n, width = qs.shape
    past = cache_kt.shape[3]
    assert past % SB_TILE == 0 and n_new % 8 == 0 and n_new <= LANES
    rows = SB_HEADS * n_new
    row = lambda b: (b, 0)
    cache = lambda b: (layer, b, 0, 0)
    return pl.pallas_call(
        functools.partial(_sb_sample_body, past=past, n_new=n_new),
        grid=(n // n_new,),
        in_specs=[pl.BlockSpec((n_new, width), row),
                  pl.BlockSpec((None, None, width, past), cache),
                  pl.BlockSpec((None, None, width, past), cache),
                  pl.BlockSpec((n_new, width), row),
                  pl.BlockSpec((n_new, width), row)],
        out_specs=pl.BlockSpec((n_new, width), row),
        out_shape=jax.ShapeDtypeStruct((n, width), F32),
        scratch_shapes=[pltpu.VMEM((SB_TILE, SB_TILE), BF16),
                        pltpu.VMEM((rows, LANES), F32),
                        pltpu.VMEM((rows, width), F32)],
        compiler_params=_params(1),
        name="sb_sample",
    )(qs, cache_kt, cache_vt, kb, vb)


def _merge_rows(x, u_ref, va_ref, sb_ref, ws_ref, bs_ref, ga_ref, gb_ref, wo_ref, cat_ref):
    tm = x.shape[0]
    width = u_ref.shape[1]
    e = width // SGU_GROUPS
    r = lax.broadcasted_iota(jnp.int32, (SGU_CHUNK, SGU_CHUNK), 0)
    c = lax.broadcasted_iota(jnp.int32, (SGU_CHUNK, SGU_CHUNK), 1)
    n_chunks = tm // SGU_CHUNK
    cat_ref[:, width:] = _rmsnorm(sb_ref[...], gb_ref[...]).astype(BF16)
    va = va_ref[...].astype(BF16)
    mixes = []
    for g in range(SGU_GROUPS):
        w_g = jnp.where(r >= c, ws_ref[g], 0.0).astype(BF16)
        chunks = jnp.concatenate([va[ch * SGU_CHUNK:(ch + 1) * SGU_CHUNK, g * e:(g + 1) * e]
                                  for ch in range(n_chunks)], axis=1)
        mixes.append(jnp.dot(w_g, chunks, preferred_element_type=F32) + bs_ref[:, g:g + 1])
    mix = jnp.concatenate([jnp.concatenate([m[:, ch * e:(ch + 1) * e] for m in mixes], axis=1)
                           for ch in range(n_chunks)], axis=0)
    cat_ref[:, 0:width] = _rmsnorm(u_ref[...] * mix, ga_ref[...]).astype(BF16)
    return x + jnp.dot(cat_ref[...], wo_ref[...], preferred_element_type=F32)


def _merge_body(x_ref, u_ref, va_ref, sb_ref, ws_ref, bs_ref, ga_ref, gb_ref, wo_ref, o_ref, cat_ref):
    o_ref[...] = _merge_rows(x_ref[...], u_ref, va_ref, sb_ref, ws_ref, bs_ref, ga_ref, gb_ref, wo_ref, cat_ref)


def _merge_specs(u, sb, d, va_layer, layer, tm):
    width, sb_width = u.shape[1], sb.shape[1]
    row = lambda i: (i, 0)
    lay3 = lambda i: (layer, 0, 0)
    return [pl.BlockSpec((tm, d), row),
            pl.BlockSpec((tm, width), row),
            pl.BlockSpec((None, tm, width), lambda i: (va_layer, i, 0)),
            pl.BlockSpec((tm, sb_width), row),
            _resident((SGU_GROUPS, SGU_CHUNK, SGU_CHUNK), lambda i: (0, 0, 0)),
            _resident((SGU_CHUNK, SGU_GROUPS), lambda i: (0, 0)),
            _resident((None, 1, width), lay3),
            _resident((None, 1, sb_width), lay3),
            _resident((None, width + sb_width, d), lay3)]


def _merge(x, u, va, va_layer, sb, w_mix, b_mix, g_a, g_b, w_out, layer, tm):
    n, d = x.shape
    return pl.pallas_call(
        _merge_body,
        grid=(n // tm,),
        in_specs=_merge_specs(u, sb, d, va_layer, layer, tm),
        out_specs=pl.BlockSpec((tm, d), lambda i: (i, 0)),
        out_shape=jax.ShapeDtypeStruct((n, d), F32),
        scratch_shapes=[pltpu.VMEM((tm, u.shape[1] + sb.shape[1]), BF16)],
        compiler_params=_params(1),
        name="merge",
    )(x, u, va, sb, w_mix, b_mix, g_a, g_b, w_out)


def _mem_kv_body(m_ref, wk_ref, wv_ref, k_ref, v_ref):
    m = m_ref[...].astype(BF16)
    k_ref[...] = jnp.dot(m, wk_ref[...], preferred_element_type=F32)
    v_ref[...] = jnp.dot(m, wv_ref[...], preferred_element_type=F32)


def _mem_kv(mem, w_mk, w_mv):
    b, m, d = mem.shape
    depth = w_mk.shape[0]
    out = jax.ShapeDtypeStruct((depth, b, m, d), F32)
    w_spec = pl.BlockSpec((None, d, d), lambda l, i: (l, 0, 0))
    o_spec = pl.BlockSpec((None, None, m, d), lambda l, i: (l, i, 0, 0))
    return pl.pallas_call(
        _mem_kv_body,
        grid=(depth, b),
        in_specs=[pl.BlockSpec((None, m, d), lambda l, i: (i, 0, 0)), w_spec, w_spec],
        out_specs=[o_spec, o_spec],
        out_shape=[out, out],
        compiler_params=_params(2),
        name="mem_kv",
    )(mem, w_mk, w_mv)


def _mem_attend_rows(x, ln_ref, wq_ref, mk_ref, mv_ref, wo_ref, oc_ref, n_streams):
    tm, d = x.shape
    dh = d // MEM_HEADS
    rpb = tm // n_streams
    h = _rmsnorm(x, ln_ref[...]).astype(BF16)
    q = (jnp.dot(h, wq_ref[...], preferred_element_type=F32) * dh ** -0.5)
    nt_dims = (((1,), (1,)), ((), ()))
    if MEM_HEADS * rpb <= LANES:
        stacked = MEM_HEADS * rpb
        own = (lax.broadcasted_iota(jnp.int32, (stacked, d), 0) // rpb
               == lax.broadcasted_iota(jnp.int32, (stacked, d), 1) // dh)
        for s in range(n_streams):
            rows = slice(s * rpb, (s + 1) * rpb)
            q_bd = jnp.where(own, jnp.concatenate([q[rows]] * MEM_HEADS, axis=0), 0.0).astype(BF16)
            k_s = jnp.concatenate([mk_ref[s, hd] for hd in range(MEM_HEADS)], axis=1).astype(BF16)
            v_s = jnp.concatenate([mv_ref[s, hd] for hd in range(MEM_HEADS)], axis=1).astype(BF16)
            sc = lax.dot_general(q_bd, k_s, nt_dims, preferred_element_type=F32)
            p = jnp.exp(sc - jnp.max(sc, axis=-1, keepdims=True))
            o = jnp.dot(p.astype(BF16), v_s, preferred_element_type=F32)
            o = jnp.where(own, o / jnp.sum(p, axis=-1, keepdims=True), 0.0)
            o_s = o[0:rpb]
            for hd in range(1, MEM_HEADS):
                o_s = o_s + o[hd * rpb:(hd + 1) * rpb]
            oc_ref[rows, :] = o_s.astype(BF16)
    else:
        q = q.astype(BF16)
        for s in range(n_streams):
            rows = slice(s * rpb, (s + 1) * rpb)
            for hd in range(MEM_HEADS):
                cols = slice(hd * dh, (hd + 1) * dh)
                k = mk_ref[s, :, cols].astype(BF16)
                v = mv_ref[s, :, cols].astype(BF16)
                sc = lax.dot_general(q[rows, cols], k, nt_dims, preferred_element_type=F32)
                p = jnp.exp(sc - jnp.max(sc, axis=-1, keepdims=True))
                o = jnp.dot(p.astype(BF16), v, preferred_element_type=F32) / jnp.sum(p, axis=-1, keepdims=True)
                oc_ref[rows, cols] = o.astype(BF16)
    return x + jnp.dot(oc_ref[...], wo_ref[...], preferred_element_type=F32)


def _mem_attn_body(x_ref, ln_ref, wq_ref, mk_ref, mv_ref, wo_ref, o_ref, oc_ref, *, n_streams):
    o_ref[...] = _mem_attend_rows(x_ref[...], ln_ref, wq_ref, mk_ref, mv_ref, wo_ref, oc_ref, n_streams)


def _mem_specs(mk, d, layer, tm, rows_per_stream):
    n_streams = max(1, tm // rows_per_stream)
    tiles_per_stream = max(1, rows_per_stream // tm)
    lay3 = lambda i: (layer, 0, 0)
    if mk.ndim == 5:
        mem_spec = pl.BlockSpec((None, n_streams) + mk.shape[2:], lambda i: (layer, i // tiles_per_stream, 0, 0, 0))
    else:
        mem_spec = pl.BlockSpec((None, n_streams) + mk.shape[2:], lambda i: (layer, i // tiles_per_stream, 0, 0))
    return n_streams, [_resident((None, 1, d), lay3), _resident((None, d, d), lay3), mem_spec, mem_spec,
                       _resident((None, d, d), lay3)]


def _mem_attn(x, ln, w_mq, mk, mv, w_mo, layer, tm, rows_per_stream):
    n, d = x.shape
    row = lambda i: (i, 0)
    n_streams, specs = _mem_specs(mk, d, layer, tm, rows_per_stream)
    return pl.pallas_call(
        functools.partial(_mem_attn_body, n_streams=n_streams),
        grid=(n // tm,),
        in_specs=[pl.BlockSpec((tm, d), row)] + specs,
        out_specs=pl.BlockSpec((tm, d), row),
        out_shape=jax.ShapeDtypeStruct((n, d), F32),
        scratch_shapes=[pltpu.VMEM((tm, d), BF16)],
        compiler_params=_params(1),
        name="mem_attn",
    )(x, ln, w_mq, mk, mv, w_mo)


def _merge_mem_body(x_ref, u_ref, va_ref, sb_ref, ws_ref, bs_ref, ga_ref, gb_ref, wo_ref,
                    ln_ref, wq_ref, mk_ref, mv_ref, wmo_ref, o_ref, cat_ref, oc_ref, *, n_streams):
    x = _merge_rows(x_ref[...], u_ref, va_ref, sb_ref, ws_ref, bs_ref, ga_ref, gb_ref, wo_ref, cat_ref)
    o_ref[...] = _mem_attend_rows(x, ln_ref, wq_ref, mk_ref, mv_ref, wmo_ref, oc_ref, n_streams)


def _merge_mem(x, u, va, va_layer, sb, w_mix, b_mix, g_a, g_b, w_out, ln, w_mq, mk, mv, w_mo, layer, tm,
               rows_per_stream):
    n, d = x.shape
    n_streams, mem_specs = _mem_specs(mk, d, layer, tm, rows_per_stream)
    return pl.pallas_call(
        functools.partial(_merge_mem_body, n_streams=n_streams),
        grid=(n // tm,),
        in_specs=_merge_specs(u, sb, d, va_layer, layer, tm) + mem_specs,
        out_specs=pl.BlockSpec((tm, d), lambda i: (i, 0)),
        out_shape=jax.ShapeDtypeStruct((n, d), F32),
        scratch_shapes=[pltpu.VMEM((tm, u.shape[1] + sb.shape[1]), BF16), pltpu.VMEM((tm, d), BF16)],
        compiler_params=_params(1),
        name="merge_mem",
    )(x, u, va, sb, w_mix, b_mix, g_a, g_b, w_out, ln, w_mq, mk, mv, w_mo)


def _ffn_body(x_ref, ln_ref, wg_ref, wu_ref, wd_ref, lnf_ref, o_ref, *, final_norm):
    x = x_ref[...]
    h = _rmsnorm(x, ln_ref[...]).astype(BF16)
    gate = jnp.dot(h, wg_ref[...], preferred_element_type=F32)
    up = jnp.dot(h, wu_ref[...], preferred_element_type=F32)
    act = (jax.nn.silu(gate) * up).astype(BF16)
    y = x + jnp.dot(act, wd_ref[...], preferred_element_type=F32)
    if final_norm:
        y = _rmsnorm(y, lnf_ref[...])
    o_ref[...] = y


def _ffn(x, ln, w_gate, w_up, w_down, ln_final, layer, tm, final_norm):
    n, d = x.shape
    d_ff = w_gate.shape[-1]
    row = lambda i: (i, 0)
    lay3 = lambda i: (layer, 0, 0)
    return pl.pallas_call(
        functools.partial(_ffn_body, final_norm=final_norm),
        grid=(n // tm,),
        in_specs=[pl.BlockSpec((tm, d), row),
                  _resident((None, 1, d), lay3),
                  _resident((None, d, d_ff), lay3),
                  _resident((None, d, d_ff), lay3),
                  _resident((None, d_ff, d), lay3),
                  _resident((1, d), lambda i: (0, 0))],
        out_specs=pl.BlockSpec((tm, d), row),
        out_shape=jax.ShapeDtypeStruct((n, d), F32),
        compiler_params=_params(1),
        name="ffn",
    )(x, ln, w_gate, w_up, w_down, ln_final)


def kernel(x_prompt, x_sample, cache_sb_k, cache_sb_v, cache_mem_k, cache_mem_v, mem_prompt, ln_mix, w_in, g_sgu_v, w_sgu, b_sgu, g_out_sgu, g_out_sb, w_out, ln_mem, w_mq, w_mk, w_mv, w_mo, ln_ffn, w_ffn_gate, w_ffn_up, w_ffn_down, ln_final):
    batch, t_len, d = x_prompt.shape
    bs, n_new, _ = x_sample.shape
    depth = w_in.shape[0]
    past = cache_sb_k.shape[2]
    n_mem = mem_prompt.shape[1]
    width = d // 2
    dh = width // SB_HEADS
    e = width // SGU_GROUPS
    assert t_len % ROW_TILE == 0 and t_len % SB_TILE == 0 and SGU_CHUNK % n_new == 0

    w_in_b, w_out_b = w_in.astype(BF16), w_out.astype(BF16)
    w_kvt_b = jnp.swapaxes(w_in_b[:, :, 3 * width:], 1, 2)
    w_mq_b, w_mk_b, w_mv_b, w_mo_b = (w.astype(BF16) for w in (w_mq, w_mk, w_mv, w_mo))
    w_g_b, w_u_b, w_d_b = (w.astype(BF16) for w in (w_ffn_gate, w_ffn_up, w_ffn_down))
    as_rows = lambda g: g.reshape(depth, 1, -1)
    ln_mix_r, ln_mem_r, ln_ffn_r = as_rows(ln_mix), as_rows(ln_mem), as_rows(ln_ffn)
    g_a_r, g_b_r = as_rows(g_out_sgu), as_rows(g_out_sb)
    ln_final_r = ln_final.reshape(1, d)

    cache_kt = cache_sb_k.transpose(0, 1, 3, 4, 2).reshape(depth, bs, width, past)
    cache_vt = cache_sb_v.transpose(0, 1, 3, 4, 2).reshape(depth, bs, width, past)
    cmem_k = cache_mem_k.transpose(0, 1, 3, 2, 4)
    cmem_v = cache_mem_v.transpose(0, 1, 3, 2, 4)

    mk_p, mv_p = _mem_kv(mem_prompt, w_mk_b, w_mv_b)

    n_p = batch * t_len
    x = x_prompt.reshape(n_p, d)
    kt_st = jnp.zeros((depth, batch, width, t_len), F32)
    vt_st = jnp.zeros((depth, batch, width, t_len), F32)
    for l in range(depth):
        u, va, qs, ktb, vtb, kt_st, vt_st = _in_proj(x, ln_mix_r, w_in_b, w_kvt_b, g_sgu_v, kt_st, vt_st, None, l, ROW_TILE_WIDE)
        sb = _sb_prompt(qs, ktb, vtb, batch, t_len)
        x = _merge_mem(x, u, va, 0, sb, w_sgu[l], b_sgu[l].T, g_a_r, g_b_r, w_out_b,
                       ln_mem_r, w_mq_b, mk_p, mv_p, w_mo_b, l, ROW_TILE, t_len)
        x = _ffn(x, ln_ffn_r, w_g_b, w_u_b, w_d_b, ln_final_r, l, ROW_TILE, l == depth - 1)
    y_prompt = x.reshape(batch, t_len, d)
    sb_k_prompt = kt_st.reshape(depth, batch, SB_HEADS, dh, t_len).transpose(0, 1, 4, 2, 3)
    sb_v_prompt = vt_st.reshape(depth, batch, SB_HEADS, dh, t_len).transpose(0, 1, 4, 2, 3)

    n_s = bs * n_new
    tm_s = min(ROW_TILE, n_s)
    tm_mem = 4 * n_new
    reps = SGU_CHUNK // n_new
    x = x_sample.reshape(n_s, d)
    k_st = jnp.zeros((depth, n_s, width), F32)
    v_st = jnp.zeros((depth, n_s, width), F32)
    va_st = jnp.zeros((depth, n_s, width), F32)
    for l in range(depth):
        u, va_st, qs, kb, vb, k_st, v_st = _in_proj(x, ln_mix_r, w_in_b, None, g_sgu_v, k_st, v_st, va_st, l, tm_s)
        sb = _sb_sample(qs, kb, vb, cache_kt, cache_vt, l, n_new)
        w_blk = jnp.einsum('ab,gts->gatbs', jnp.eye(reps, dtype=F32), w_sgu[l][:, :n_new, :n_new])
        w_blk = w_blk.reshape(SGU_GROUPS, SGU_CHUNK, SGU_CHUNK)
        b_blk = jnp.tile(b_sgu[l][:, :n_new], (1, reps)).T
        x = _merge(x, u, va_st, l, sb, w_blk, b_blk, g_a_r, g_b_r, w_out_b, l, tm_s)
        x = _mem_attn(x, ln_mem_r, w_mq_b, cmem_k, cmem_v, w_mo_b, l, tm_mem, n_new)
        x = _ffn(x, ln_ffn_r, w_g_b, w_u_b, w_d_b, ln_final_r, l, tm_s, l == depth - 1)
    y_sample = x.reshape(bs, n_new, d)

    return (y_prompt, y_sample, sb_k_prompt, sb_v_prompt,
            mk_p.reshape(depth, batch, n_mem, MEM_HEADS, d // MEM_HEADS),
            mv_p.reshape(depth, batch, n_mem, MEM_HEADS, d // MEM_HEADS),
            k_st.reshape(depth, bs, n_new, SB_HEADS, dh),
            v_st.reshape(depth, bs, n_new, SB_HEADS, dh),
            va_st.reshape(depth, bs, n_new, SGU_GROUPS, e))
```

```python
import functools

import jax
import jax.numpy as jnp
from jax import lax
from jax.experimental import pallas as pl
from jax.experimental.pallas import tpu as pltpu

F32 = jnp.float32
BF16 = jnp.bfloat16
EPS = 1e-6

LANES = 128
SGU_CHUNK = 128
SGU_GROUPS = 4
SB_HEADS = 8
MEM_HEADS = 4
ROW_TILE = 512
ROW_TILE_WIDE = 1024
SB_TILE = 256
SB_GROUP = 4
SB_SPLIT = 4
LOG2E = 1.4426950408889634
SB_EXIT_LOG = -106.0
VMEM_LIMIT = 56 * 1024 * 1024


def _params(n_axes):
    return pltpu.CompilerParams(dimension_semantics=("parallel",) * n_axes,
                                vmem_limit_bytes=VMEM_LIMIT)


def _resident(block_shape, index_map):
    return pl.BlockSpec(block_shape, index_map, pipeline_mode=pl.Buffered(1))


def _rmsnorm(x, g):
    return x * lax.rsqrt(jnp.mean(x * x, axis=-1, keepdims=True) + EPS) * g


def _in_proj_body(x_ref, ln_ref, w_ref, gv_ref, *refs, width, q_scale, n_stacks, kv_transposed):
    n_extra = 1 if kv_transposed else 0
    u_ref, va_ref, qs_ref, kb_ref, vb_ref, k_ref, v_ref = refs[n_extra + n_stacks:n_extra + n_stacks + 7]
    h = _rmsnorm(x_ref[...], ln_ref[...]).astype(BF16)

    def proj(j):
        return jnp.dot(h, w_ref[:, j * width:(j + 1) * width], preferred_element_type=F32)

    u_ref[...] = jax.nn.gelu(proj(0))
    va = jax.nn.gelu(proj(1))
    e = width // SGU_GROUPS
    for g in range(SGU_GROUPS):
        va_ref[:, g * e:(g + 1) * e] = _rmsnorm(va[:, g * e:(g + 1) * e], gv_ref[g:g + 1, :])
    qs_ref[...] = (proj(2) * q_scale).astype(BF16)
    if kv_transposed:
        kvt = lax.dot_general(refs[0][...], h, (((1,), (1,)), ((), ())), preferred_element_type=F32)
        k, v = kvt[:width], kvt[width:]
    else:
        k, v = proj(3), proj(4)
    k_ref[...] = k
    kb_ref[...] = k.astype(BF16)
    v_ref[...] = v
    vb_ref[...] = v.astype(BF16)


def _in_proj(x, ln, w_in, w_kvt, g_v, k_stack, v_stack, va_stack, layer, tm):
    n, d = x.shape
    width = d // 2
    in_w = w_in.shape[-1]
    row = lambda i: (i, 0)
    lay3 = lambda i: (layer, 0, 0)
    kv_transposed = w_kvt is not None
    f32_out = jax.ShapeDtypeStruct((n, width), F32)
    bf_out = jax.ShapeDtypeStruct((n, width), BF16)
    out_spec = pl.BlockSpec((tm, width), row)
    if kv_transposed:
        _, batch, _, t_len = k_stack.shape
        tpb = t_len // tm
        extra, extra_specs = [w_kvt], [_resident((None, 2 * width, d), lay3)]
        stacks = [k_stack, v_stack]
        stack_spec = pl.BlockSpec((None, None, width, tm), lambda i: (layer, i // tpb, 0, i % tpb))
        kb_shape = jax.ShapeDtypeStruct((batch, width, t_len), BF16)
        kb_spec = pl.BlockSpec((None, width, tm), lambda i: (i // tpb, 0, i % tpb))
        va_shape = jax.ShapeDtypeStruct((1, n, width), F32)
        va_spec = pl.BlockSpec((None, tm, width), lambda i: (0, i, 0))
        aliases = {5: 5, 6: 6}
    else:
        extra, extra_specs = [], []
        stacks = [k_stack, v_stack, va_stack]
        stack_spec = pl.BlockSpec((None, tm, width), lambda i: (layer, i, 0))
        kb_shape, kb_spec = bf_out, out_spec
        va_shape, va_spec = jax.ShapeDtypeStruct(va_stack.shape, F32), stack_spec
        aliases = {4: 5, 5: 6, 6: 1}
    return pl.pallas_call(
        functools.partial(_in_proj_body, width=width, q_scale=(width // SB_HEADS) ** -0.5, n_stacks=len(stacks),
                          kv_transposed=kv_transposed),
        grid=(n // tm,),
        in_specs=[pl.BlockSpec((tm, d), row),
                  _resident((None, 1, d), lay3),
                  _resident((None, d, in_w), lay3),
                  _resident((None, SGU_GROUPS, width // SGU_GROUPS), lay3)]
                 + extra_specs + [pl.BlockSpec(memory_space=pl.ANY)] * len(stacks),
        out_specs=[out_spec, va_spec, out_spec, kb_spec, kb_spec, stack_spec, stack_spec],
        out_shape=[f32_out, va_shape, bf_out, kb_shape, kb_shape,
                   jax.ShapeDtypeStruct(k_stack.shape, F32), jax.ShapeDtypeStruct(v_stack.shape, F32)],
        input_output_aliases=aliases,
        compiler_params=_params(1),
        name="in_proj",
    )(x, ln, w_in, g_v, *extra, *stacks)


def _neg_log_keep(z):
    return jnp.maximum(z, 0.0) + jnp.log(1.0 + jnp.exp2(jnp.abs(z) * (-LOG2E)))


def _neg_suffix_matrix(n):
    r = lax.broadcasted_iota(jnp.int32, (n, n), 0)
    c = lax.broadcasted_iota(jnp.int32, (n, n), 1)
    return jnp.where(r >= c, -1.0, 0.0).astype(BF16)


def _sb_prompt_body(q_ref, kt_ref, vt_ref, o_ref, nuo_ref, qm_ref, z_ref, l_ref, a_ref, carry_ref, acc_ref, *, dh):
    t_len, gw = q_ref.shape
    tile = SB_TILE
    n_heads = gw // dh
    rows = n_heads * tile
    part_rows = rows // SB_SPLIT
    slabs = part_rows // LANES
    nuo_ref[...] = _neg_suffix_matrix(tile)
    lane_head = lax.broadcasted_iota(jnp.int32, (tile, gw), 1) // dh

    def stage_logits(c, masked, slot, part):
        k0 = pl.multiple_of(c * tile, tile)
        base = part * part_rows
        return jnp.dot(qm_ref[base:base + part_rows, :], kt_ref[:, pl.ds(k0, tile)], preferred_element_type=F32)

    def stage_keep(c, masked, slot, part, z_all):
        base = part * part_rows
        for sl in range(slabs):
            r0 = base + sl * LANES
            z = z_all[sl * LANES:(sl + 1) * LANES]
            nlk = _neg_log_keep(z)
            if masked:
                qpos = (r0 % tile) + lax.broadcasted_iota(jnp.int32, (LANES, tile), 0)
                kpos = lax.broadcasted_iota(jnp.int32, (LANES, tile), 1)
                nlk = jnp.where(kpos < qpos, nlk, 0.0)
            z_ref[slot, r0:r0 + LANES, :] = z
            l_ref[slot, r0:r0 + LANES, :] = nlk.astype(BF16)
        return jnp.dot(l_ref[slot, base:base + part_rows, :], nuo_ref[...], preferred_element_type=F32)

    def stage_weights(c, masked, slot, part, r_all):
        k0 = pl.multiple_of(c * tile, tile)
        base = part * part_rows
        for sl in range(slabs):
            r0 = base + sl * LANES
            r = r_all[sl * LANES:(sl + 1) * LANES]
            total = jnp.broadcast_to(r[:, 0:1], (LANES, LANES))
            if masked:
                cum = r
                carry_ref[r0:r0 + LANES, :] = total
            else:
                carry = carry_ref[r0:r0 + LANES, :]
                cum = r + jnp.concatenate([carry] * (tile // LANES), axis=1)
                carry_ref[r0:r0 + LANES, :] = carry + total
            a = jnp.exp(z_ref[slot, r0:r0 + LANES, :] + cum)
            if masked:
                qpos = (r0 % tile) + lax.broadcasted_iota(jnp.int32, (LANES, tile), 0)
                kpos = lax.broadcasted_iota(jnp.int32, (LANES, tile), 1)
                a = jnp.where(kpos < qpos, a, 0.0)
            a_ref[slot, r0:r0 + LANES, :] = a.astype(BF16)
        o_part = lax.dot_general(a_ref[slot, base:base + part_rows, :], vt_ref[:, pl.ds(k0, tile)],
                                 (((1,), (1,)), ((), ())), preferred_element_type=F32)
        if masked:
            acc_ref[base:base + part_rows, :] = o_part
        else:
            acc_ref[base:base + part_rows, :] += o_part

    def fold(*tiles):
        items = [(c, masked, slot, part) for part in range(SB_SPLIT) for (c, masked, slot) in tiles]
        logits, sums = {}, {}
        for step in range(len(items) + 3):
            if step < len(items):
                logits[step] = stage_logits(*items[step])
            if 0 <= step - 1 < len(items):
                sums[step - 1] = stage_keep(*items[step - 1], logits.pop(step - 1))
            if 0 <= step - 3 < len(items):
                stage_weights(*items[step - 3], sums.pop(step - 3))

    def begin(q0):
        q = q_ref[pl.ds(q0, tile), :].astype(F32)
        for h in range(n_heads):
            qm_ref[h * tile:(h + 1) * tile, :] = jnp.where(lane_head == h, q, 0.0).astype(BF16)

    def finish(q0):
        out = acc_ref[0:tile, :]
        for h in range(1, n_heads):
            out = jnp.where(lane_head == h, acc_ref[h * tile:(h + 1) * tile, :], out)
        o_ref[pl.ds(q0, tile), :] = out

    begin(0)
    fold((0, True, 0))
    finish(0)

    def q_tile(j, _):
        q0 = pl.multiple_of(j * tile, tile)
        begin(q0)
        fold((j, True, 0), (j - 1, False, 1))

        def more(state):
            c, max_carry = state
            return (c >= 0) & (max_carry >= SB_EXIT_LOG)

        def earlier(state):
            c, _ = state
            fold((c, False, 0))
            return c - 1, jnp.max(carry_ref[...])

        lax.while_loop(more, earlier, (j - 2, jnp.max(carry_ref[...])))
        finish(q0)
        return 0

    lax.fori_loop(1, t_len // tile, q_tile, 0)


def _sb_prompt(qs, ktb, vtb, batch, t_len):
    n, width = qs.shape
    dh = width // SB_HEADS
    gw = SB_GROUP * dh
    groups = width // gw
    rows = SB_GROUP * SB_TILE
    blk = pl.BlockSpec((t_len, gw), lambda b, g: (b, g))
    blk_t = pl.BlockSpec((None, gw, t_len), lambda b, g: (b, g, 0))
    return pl.pallas_call(
        functools.partial(_sb_prompt_body, dh=dh),
        grid=(batch, groups),
        in_specs=[blk, blk_t, blk_t],
        out_specs=blk,
        out_shape=jax.ShapeDtypeStruct((n, width), F32),
        scratch_shapes=[pltpu.VMEM((SB_TILE, SB_TILE), BF16),
                        pltpu.VMEM((rows, gw), BF16),
                        pltpu.VMEM((2, rows, SB_TILE), F32),
                        pltpu.VMEM((2, rows, SB_TILE), BF16),
                        pltpu.VMEM((2, rows, SB_TILE), BF16),
                        pltpu.VMEM((rows, LANES), F32),
                        pltpu.VMEM((rows, gw), F32)],
        compiler_params=_params(2),
        name="sb_prompt",
    )(qs, ktb, vtb)


def _sb_sample_body(q_ref, kc_ref, vc_ref, kn_ref, vn_ref, o_ref, nsm_ref, carry_ref, acc_ref, *, past, n_new):
    width = q_ref.shape[1]
    dh = width // SB_HEADS
    rows = SB_HEADS * n_new
    tile = SB_TILE
    nsm_ref[...] = _neg_suffix_matrix(tile)

    q_rep = jnp.concatenate([q_ref[...].astype(F32)] * SB_HEADS, axis=0)
    own = (lax.broadcasted_iota(jnp.int32, (rows, width), 0) // n_new
           == lax.broadcasted_iota(jnp.int32, (rows, width), 1) // dh)
    q_bd = jnp.where(own, q_rep, 0.0).astype(BF16)

    def fold(kt_blk, vt_blk, valid):
        nk = kt_blk.shape[1]
        z = jnp.dot(q_bd, kt_blk, preferred_element_type=F32)
        nlk = _neg_log_keep(z)
        if valid is not None:
            nlk = jnp.where(valid, nlk, 0.0)
        r = jnp.dot(nlk.astype(BF16), nsm_ref[0:nk, 0:nk], preferred_element_type=F32)
        carry = carry_ref[...]
        a = jnp.exp(z + r + jnp.concatenate([carry] * (nk // LANES), axis=1))
        if valid is not None:
            a = jnp.where(valid, a, 0.0)
        carry_ref[...] = carry + jnp.broadcast_to(r[:, 0:1], (rows, LANES))
        acc_ref[...] += lax.dot_general(a.astype(BF16), vt_blk, (((1,), (1,)), ((), ())),
                                        preferred_element_type=F32)

    zeros = jnp.zeros((width, LANES - n_new), F32)
    kt_new = jnp.concatenate([kn_ref[...].astype(F32).T, zeros], axis=1).astype(BF16)
    vt_new = jnp.concatenate([vn_ref[...].astype(F32).T, zeros], axis=1).astype(BF16)
    valid_new = (lax.broadcasted_iota(jnp.int32, (rows, LANES), 1)
                 < lax.broadcasted_iota(jnp.int32, (rows, LANES), 0) % n_new)
    carry_ref[...] = jnp.zeros_like(carry_ref)
    acc_ref[...] = jnp.zeros_like(acc_ref)
    fold(kt_new, vt_new, valid_new)

    def more(state):
        c, max_carry = state
        return (c >= 0) & (max_carry >= SB_EXIT_LOG)

    def earlier(state):
        c, _ = state
        k0 = pl.multiple_of(c * tile, tile)
        fold(kc_ref[:, pl.ds(k0, tile)].astype(BF16), vc_ref[:, pl.ds(k0, tile)].astype(BF16), None)
        return c - 1, jnp.max(carry_ref[...])

    lax.while_loop(more, earlier, (past // tile - 1, jnp.max(carry_ref[...])))
    o_all = jnp.where(own, acc_ref[...], 0.0)
    out = o_all[0:n_new, :]
    for h in range(1, SB_HEADS):
        out = out + o_all[h * n_new:(h + 1) * n_new, :]
    o_ref[...] = out


def _sb_sample(qs, kb, vb, cache_kt, cache_vt, layer, n_new):
    n, width = qs.shape
    past = cache_kt.shape[3]
    assert past % SB_TILE == 0 and n_new % 8 == 0 and n_new <= LANES
    rows = SB_HEADS * n_new
    row = lambda b: (b, 0)
    cache = lambda b: (layer, b, 0, 0)
    return pl.pallas_call(
        functools.partial(_sb_sample_body, past=past, n_new=n_new),
        grid=(n // n_new,),
        in_specs=[pl.BlockSpec((n_new, width), row),
                  pl.BlockSpec((None, None, width, past), cache),
                  pl.BlockSpec((None, None, width, past), cache),
                  pl.BlockSpec((n_new, width), row),
                  pl.BlockSpec((n_new, width), row)],
        out_specs=pl.BlockSpec((n_new, width), row),
        out_shape=jax.ShapeDtypeStruct((n, width), F32),
        scratch_shapes=[pltpu.VMEM((SB_TILE, SB_TILE), BF16),
                        pltpu.VMEM((rows, LANES), F32),
                        pltpu.VMEM((rows, width), F32)],
        compiler_params=_params(1),
        name="sb_sample",
    )(qs, cache_kt, cache_vt, kb, vb)


def _merge_rows(x, u_ref, va_ref, sb_ref, ws_ref, bs_ref, ga_ref, gb_ref, wo_ref, cat_ref):
    tm = x.shape[0]
    width = u_ref.shape[1]
    e = width // SGU_GROUPS
    r = lax.broadcasted_iota(jnp.int32, (SGU_CHUNK, SGU_CHUNK), 0)
    c = lax.broadcasted_iota(jnp.int32, (SGU_CHUNK, SGU_CHUNK), 1)
    n_chunks = tm // SGU_CHUNK
    cat_ref[:, width:] = _rmsnorm(sb_ref[...], gb_ref[...]).astype(BF16)
    va = va_ref[...].astype(BF16)
    mixes = []
    for g in range(SGU_GROUPS):
        w_g = jnp.where(r >= c, ws_ref[g], 0.0).astype(BF16)
        chunks = jnp.concatenate([va[ch * SGU_CHUNK:(ch + 1) * SGU_CHUNK, g * e:(g + 1) * e]
                                  for ch in range(n_chunks)], axis=1)
        mixes.append(jnp.dot(w_g, chunks, preferred_element_type=F32) + bs_ref[:, g:g + 1])
    mix = jnp.concatenate([jnp.concatenate([m[:, ch * e:(ch + 1) * e] for m in mixes], axis=1)
                           for ch in range(n_chunks)], axis=0)
    cat_ref[:, 0:width] = _rmsnorm(u_ref[...] * mix, ga_ref[...]).astype(BF16)
    return x + jnp.dot(cat_ref[...], wo_ref[...], preferred_element_type=F32)


def _merge_body(x_ref, u_ref, va_ref, sb_ref, ws_ref, bs_ref, ga_ref, gb_ref, wo_ref, o_ref, cat_ref):
    o_ref[...] = _merge_rows(x_ref[...], u_ref, va_ref, sb_ref, ws_ref, bs_ref, ga_ref, gb_ref, wo_ref, cat_ref)


def _merge_specs(u, sb, d, va_layer, layer, tm):
    width, sb_width = u.shape[1], sb.shape[1]
    row = lambda i: (i, 0)
    lay3 = lambda i: (layer, 0, 0)
    return [pl.BlockSpec((tm, d), row),
            pl.BlockSpec((tm, width), row),
            pl.BlockSpec((None, tm, width), lambda i: (va_layer, i, 0)),
            pl.BlockSpec((tm, sb_width), row),
            _resident((SGU_GROUPS, SGU_CHUNK, SGU_CHUNK), lambda i: (0, 0, 0)),
            _resident((SGU_CHUNK, SGU_GROUPS), lambda i: (0, 0)),
            _resident((None, 1, width), lay3),
            _resident((None, 1, sb_width), lay3),
            _resident((None, width + sb_width, d), lay3)]


def _merge(x, u, va, va_layer, sb, w_mix, b_mix, g_a, g_b, w_out, layer, tm):
    n, d = x.shape
    return pl.pallas_call(
        _merge_body,
        grid=(n // tm,),
        in_specs=_merge_specs(u, sb, d, va_layer, layer, tm),
        out_specs=pl.BlockSpec((tm, d), lambda i: (i, 0)),
        out_shape=jax.ShapeDtypeStruct((n, d), F32),
        scratch_shapes=[pltpu.VMEM((tm, u.shape[1] + sb.shape[1]), BF16)],
        compiler_params=_params(1),
        name="merge",
    )(x, u, va, sb, w_mix, b_mix, g_a, g_b, w_out)


def _mem_kv_body(m_ref, wk_ref, wv_ref, k_ref, v_ref):
    m = m_ref[...].astype(BF16)
    k_ref[...] = jnp.dot(m, wk_ref[...], preferred_element_type=F32)
    v_ref[...] = jnp.dot(m, wv_ref[...], preferred_element_type=F32)


def _mem_kv(mem, w_mk, w_mv):
    b, m, d = mem.shape
    depth = w_mk.shape[0]
    out = jax.ShapeDtypeStruct((depth, b, m, d), F32)
    w_spec = pl.BlockSpec((None, d, d), lambda l, i: (l, 0, 0))
    o_spec = pl.BlockSpec((None, None, m, d), lambda l, i: (l, i, 0, 0))
    return pl.pallas_call(
        _mem_kv_body,
        grid=(depth, b),
        in_specs=[pl.BlockSpec((None, m, d), lambda l, i: (i, 0, 0)), w_spec, w_spec],
        out_specs=[o_spec, o_spec],
        out_shape=[out, out],
        compiler_params=_params(2),
        name="mem_kv",
    )(mem, w_mk, w_mv)


def _mem_attend_rows(x, ln_ref, wq_ref, mk_ref, mv_ref, wo_ref, oc_ref, n_streams):
    tm, d = x.shape
    dh = d // MEM_HEADS
    rpb = tm // n_streams
    h = _rmsnorm(x, ln_ref[...]).astype(BF16)
    q = (jnp.dot(h, wq_ref[...], preferred_element_type=F32) * dh ** -0.5)
    nt_dims = (((1,), (1,)), ((), ()))
    if MEM_HEADS * rpb <= LANES:
        stacked = MEM_HEADS * rpb
        own = (lax.broadcasted_iota(jnp.int32, (stacked, d), 0) // rpb
               == lax.broadcasted_iota(jnp.int32, (stacked, d), 1) // dh)
        for s in range(n_streams):
            rows = slice(s * rpb, (s + 1) * rpb)
            q_bd = jnp.where(own, jnp.concatenate([q[rows]] * MEM_HEADS, axis=0), 0.0).astype(BF16)
            k_s = jnp.concatenate([mk_ref[s, hd] for hd in range(MEM_HEADS)], axis=1).astype(BF16)
            v_s = jnp.concatenate([mv_ref[s, hd] for hd in range(MEM_HEADS)], axis=1).astype(BF16)
            sc = lax.dot_general(q_bd, k_s, nt_dims, preferred_element_type=F32)
            p = jnp.exp(sc - jnp.max(sc, axis=-1, keepdims=True))
            o = jnp.dot(p.astype(BF16), v_s, preferred_element_type=F32)
            o = jnp.where(own, o / jnp.sum(p, axis=-1, keepdims=True), 0.0)
            o_s = o[0:rpb]
            for hd in range(1, MEM_HEADS):
                o_s = o_s + o[hd * rpb:(hd + 1) * rpb]
            oc_ref[rows, :] = o_s.astype(BF16)
    else:
        q = q.astype(BF16)
        for s in range(n_streams):
            rows = slice(s * rpb, (s + 1) * rpb)
            for hd in range(MEM_HEADS):
                cols = slice(hd * dh, (hd + 1) * dh)
                k = mk_ref[s, :, cols].astype(BF16)
                v = mv_ref[s, :, cols].astype(BF16)
                sc = lax.dot_general(q[rows, cols], k, nt_dims, preferred_element_type=F32)
                p = jnp.exp(sc - jnp.max(sc, axis=-1, keepdims=True))
                o = jnp.dot(p.astype(BF16), v, preferred_element_type=F32) / jnp.sum(p, axis=-1, keepdims=True)
                oc_ref[rows, cols] = o.astype(BF16)
    return x + jnp.dot(oc_ref[...], wo_ref[...], preferred_element_type=F32)


def _mem_attn_body(x_ref, ln_ref, wq_ref, mk_ref, mv_ref, wo_ref, o_ref, oc_ref, *, n_streams):
    o_ref[...] = _mem_attend_rows(x_ref[...], ln_ref, wq_ref, mk_ref, mv_ref, wo_ref, oc_ref, n_streams)


def _mem_specs(mk, d, layer, tm, rows_per_stream):
    n_streams = max(1, tm // rows_per_stream)
    tiles_per_stream = max(1, rows_per_stream // tm)
    lay3 = lambda i: (layer, 0, 0)
    if mk.ndim == 5:
        mem_spec = pl.BlockSpec((None, n_streams) + mk.shape[2:], lambda i: (layer, i // tiles_per_stream, 0, 0, 0))
    else:
        mem_spec = pl.BlockSpec((None, n_streams) + mk.shape[2:], lambda i: (layer, i // tiles_per_stream, 0, 0))
    return n_streams, [_resident((None, 1, d), lay3), _resident((None, d, d), lay3), mem_spec, mem_spec,
                       _resident((None, d, d), lay3)]


def _mem_attn(x, ln, w_mq, mk, mv, w_mo, layer, tm, rows_per_stream):
    n, d = x.shape
    row = lambda i: (i, 0)
    n_streams, specs = _mem_specs(mk, d, layer, tm, rows_per_stream)
    return pl.pallas_call(
        functools.partial(_mem_attn_body, n_streams=n_streams),
        grid=(n // tm,),
        in_specs=[pl.BlockSpec((tm, d), row)] + specs,
        out_specs=pl.BlockSpec((tm, d), row),
        out_shape=jax.ShapeDtypeStruct((n, d), F32),
        scratch_shapes=[pltpu.VMEM((tm, d), BF16)],
        compiler_params=_params(1),
        name="mem_attn",
    )(x, ln, w_mq, mk, mv, w_mo)


def _merge_mem_body(x_ref, u_ref, va_ref, sb_ref, ws_ref, bs_ref, ga_ref, gb_ref, wo_ref,
                    ln_ref, wq_ref, mk_ref, mv_ref, wmo_ref, o_ref, cat_ref, oc_ref, *, n_streams):
    x = _merge_rows(x_ref[...], u_ref, va_ref, sb_ref, ws_ref, bs_ref, ga_ref, gb_ref, wo_ref, cat_ref)
    o_ref[...] = _mem_attend_rows(x, ln_ref, wq_ref, mk_ref, mv_ref, wmo_ref, oc_ref, n_streams)


def _merge_mem(x, u, va, va_layer, sb, w_mix, b_mix, g_a, g_b, w_out, ln, w_mq, mk, mv, w_mo, layer, tm,
               rows_per_stream):
    n, d = x.shape
    n_streams, mem_specs = _mem_specs(mk, d, layer, tm, rows_per_stream)
    return pl.pallas_call(
        functools.partial(_merge_mem_body, n_streams=n_streams),
        grid=(n // tm,),
        in_specs=_merge_specs(u, sb, d, va_layer, layer, tm) + mem_specs,
        out_specs=pl.BlockSpec((tm, d), lambda i: (i, 0)),
        out_shape=jax.ShapeDtypeStruct((n, d), F32),
        scratch_shapes=[pltpu.VMEM((tm, u.shape[1] + sb.shape[1]), BF16), pltpu.VMEM((tm, d), BF16)],
        compiler_params=_params(1),
        name="merge_mem",
    )(x, u, va, sb, w_mix, b_mix, g_a, g_b, w_out, ln, w_mq, mk, mv, w_mo)


def _ffn_body(x_ref, ln_ref, wg_ref, wu_ref, wd_ref, lnf_ref, o_ref, *, final_norm):
    x = x_ref[...]
    h = _rmsnorm(x, ln_ref[...]).astype(BF16)
    gate = jnp.dot(h, wg_ref[...], preferred_element_type=F32)
    up = jnp.dot(h, wu_ref[...], preferred_element_type=F32)
    act = (jax.nn.silu(gate) * up).astype(BF16)
    y = x + jnp.dot(act, wd_ref[...], preferred_element_type=F32)
    if final_norm:
        y = _rmsnorm(y, lnf_ref[...])
    o_ref[...] = y


def _ffn(x, ln, w_gate, w_up, w_down, ln_final, layer, tm, final_norm):
    n, d = x.shape
    d_ff = w_gate.shape[-1]
    row = lambda i: (i, 0)
    lay3 = lambda i: (layer, 0, 0)
    return pl.pallas_call(
        functools.partial(_ffn_body, final_norm=final_norm),
        grid=(n // tm,),
        in_specs=[pl.BlockSpec((tm, d), row),
                  _resident((None, 1, d), lay3),
                  _resident((None, d, d_ff), lay3),
                  _resident((None, d, d_ff), lay3),
                  _resident((None, d_ff, d), lay3),
                  _resident((1, d), lambda i: (0, 0))],
        out_specs=pl.BlockSpec((tm, d), row),
        out_shape=jax.ShapeDtypeStruct((n, d), F32),
        compiler_params=_params(1),
        name="ffn",
    )(x, ln, w_gate, w_up, w_down, ln_final)


def kernel(x_prompt, x_sample, cache_sb_k, cache_sb_v, cache_mem_k, cache_mem_v, mem_prompt, ln_mix, w_in, g_sgu_v, w_sgu, b_sgu, g_out_sgu, g_out_sb, w_out, ln_mem, w_mq, w_mk, w_mv, w_mo, ln_ffn, w_ffn_gate, w_ffn_up, w_ffn_down, ln_final):
    batch, t_len, d = x_prompt.shape
    bs, n_new, _ = x_sample.shape
    depth = w_in.shape[0]
    past = cache_sb_k.shape[2]
    n_mem = mem_prompt.shape[1]
    width = d // 2
    dh = width // SB_HEADS
    e = width // SGU_GROUPS
    assert t_len % ROW_TILE == 0 and t_len % SB_TILE == 0 and SGU_CHUNK % n_new == 0

    w_in_b, w_out_b = w_in.astype(BF16), w_out.astype(BF16)
    w_kvt_b = jnp.swapaxes(w_in_b[:, :, 3 * width:], 1, 2)
    w_mq_b, w_mk_b, w_mv_b, w_mo_b = (w.astype(BF16) for w in (w_mq, w_mk, w_mv, w_mo))
    w_g_b, w_u_b, w_d_b = (w.astype(BF16) for w in (w_ffn_gate, w_ffn_up, w_ffn_down))
    as_rows = lambda g: g.reshape(depth, 1, -1)
    ln_mix_r, ln_mem_r, ln_ffn_r = as_rows(ln_mix), as_rows(ln_mem), as_rows(ln_ffn)
    g_a_r, g_b_r = as_rows(g_out_sgu), as_rows(g_out_sb)
    ln_final_r = ln_final.reshape(1, d)

    cache_kt = cache_sb_k.transpose(0, 1, 3, 4, 2).reshape(depth, bs, width, past)
    cache_vt = cache_sb_v.transpose(0, 1, 3, 4, 2).reshape(depth, bs, width, past)
    cmem_k = cache_mem_k.transpose(0, 1, 3, 2, 4)
    cmem_v = cache_mem_v.transpose(0, 1, 3, 2, 4)

    mk_p, mv_p = _mem_kv(mem_prompt, w_mk_b, w_mv_b)

    n_p = batch * t_len
    x = x_prompt.reshape(n_p, d)
    kt_st = jnp.zeros((depth, batch, width, t_len), F32)
    vt_st = jnp.zeros((depth, batch, width, t_len), F32)
    for l in range(depth):
        u, va, qs, ktb, vtb, kt_st, vt_st = _in_proj(x, ln_mix_r, w_in_b, w_kvt_b, g_sgu_v, kt_st, vt_st, None, l, ROW_TILE_WIDE)
        sb = _sb_prompt(qs, ktb, vtb, batch, t_len)
        x = _merge_mem(x, u, va, 0, sb, w_sgu[l], b_sgu[l].T, g_a_r, g_b_r, w_out_b,
                       ln_mem_r, w_mq_b, mk_p, mv_p, w_mo_b, l, ROW_TILE, t_len)
        x = _ffn(x, ln_ffn_r, w_g_b, w_u_b, w_d_b, ln_final_r, l, ROW_TILE, l == depth - 1)
    y_prompt = x.reshape(batch, t_len, d)
    sb_k_prompt = kt_st.reshape(depth, batch, SB_HEADS, dh, t_len).transpose(0, 1, 4, 2, 3)
    sb_v_prompt = vt_st.reshape(depth, batch, SB_HEADS, dh, t_len).transpose(0, 1, 4, 2, 3)

    n_s = bs * n_new
    tm_s = min(ROW_TILE, n_s)
    tm_mem = 4 * n_new
    reps = SGU_CHUNK // n_new
    x = x_sample.reshape(n_s, d)
    k_st = jnp.zeros((depth, n_s, width), F32)
    v_st = jnp.zeros((depth, n_s, width), F32)
    va_st = jnp.zeros((depth, n_s, width), F32)
    for l in range(depth):
        u, va_st, qs, kb, vb, k_st, v_st = _in_proj(x, ln_mix_r, w_in_b, None, g_sgu_v, k_st, v_st, va_st, l, tm_s)
        sb = _sb_sample(qs, kb, vb, cache_kt, cache_vt, l, n_new)
        w_blk = jnp.einsum('ab,gts->gatbs', jnp.eye(reps, dtype=F32), w_sgu[l][:, :n_new, :n_new])
        w_blk = w_blk.reshape(SGU_GROUPS, SGU_CHUNK, SGU_CHUNK)
        b_blk = jnp.tile(b_sgu[l][:, :n_new], (1, reps)).T
        x = _merge(x, u, va_st, l, sb, w_blk, b_blk, g_a_r, g_b_r, w_out_b, l, tm_s)
        x = _mem_attn(x, ln_mem_r, w_mq_b, cmem_k, cmem_v, w_mo_b, l, tm_mem, n_new)
        x = _ffn(x, ln_ffn_r, w_g_b, w_u_b, w_d_b, ln_final_r, l, tm_s, l == depth - 1)
    y_sample = x.reshape(bs, n_new, d)

    return (y_prompt, y_sample, sb_k_prompt, sb_v_prompt,
            mk_p.reshape(depth, batch, n_mem, MEM_HEADS, d // MEM_HEADS),
            mv_p.reshape(depth, batch, n_mem, MEM_HEADS, d // MEM_HEADS),
            k_st.reshape(depth, bs, n_new, SB_HEADS, dh),
            v_st.reshape(depth, bs, n_new, SB_HEADS, dh),
            va_st.reshape(depth, bs, n_new, SGU_GROUPS, e))
```

```python
import functools

import jax
import jax.numpy as jnp
from jax import lax
from jax.experimental import pallas as pl
from jax.experimental.pallas import tpu as pltpu

F32 = jnp.float32
BF16 = jnp.bfloat16
EPS = 1e-6

LANES = 128
SGU_CHUNK = 128
SGU_GROUPS = 4
SB_HEADS = 8
MEM_HEADS = 4
ROW_TILE = 512
ROW_TILE_WIDE = 1024
SB_TILE = 256
SB_GROUP = 4
SB_SPLIT = 4
LOG2E = 1.4426950408889634
SB_SAMPLE_STREAMS = 2
SB_EXIT_LOG = -106.0
VMEM_LIMIT = 56 * 1024 * 1024


def _params(n_axes):
    return pltpu.CompilerParams(dimension_semantics=("parallel",) * n_axes,
                                vmem_limit_bytes=VMEM_LIMIT)


def _resident(block_shape, index_map):
    return pl.BlockSpec(block_shape, index_map, pipeline_mode=pl.Buffered(1))


def _rmsnorm(x, g):
    return x * lax.rsqrt(jnp.mean(x * x, axis=-1, keepdims=True) + EPS) * g


def _in_proj_body(x_ref, ln_ref, w_ref, gv_ref, *refs, width, q_scale, n_stacks, kv_transposed):
    n_extra = 1 if kv_transposed else 0
    u_ref, va_ref, qs_ref, kb_ref, vb_ref, k_ref, v_ref = refs[n_extra + n_stacks:n_extra + n_stacks + 7]
    h = _rmsnorm(x_ref[...], ln_ref[...]).astype(BF16)

    def proj(j):
        return jnp.dot(h, w_ref[:, j * width:(j + 1) * width], preferred_element_type=F32)

    u_ref[...] = jax.nn.gelu(proj(0))
    va = jax.nn.gelu(proj(1))
    e = width // SGU_GROUPS
    for g in range(SGU_GROUPS):
        va_ref[:, g * e:(g + 1) * e] = _rmsnorm(va[:, g * e:(g + 1) * e], gv_ref[g:g + 1, :])
    qs_ref[...] = (proj(2) * q_scale).astype(BF16)
    if kv_transposed:
        kvt = lax.dot_general(refs[0][...], h, (((1,), (1,)), ((), ())), preferred_element_type=F32)
        k, v = kvt[:width], kvt[width:]
    else:
        k, v = proj(3), proj(4)
    k_ref[...] = k
    kb_ref[...] = k.astype(BF16)
    v_ref[...] = v
    vb_ref[...] = v.astype(BF16)


def _in_proj(x, ln, w_in, w_kvt, g_v, k_stack, v_stack, va_stack, layer, tm):
    n, d = x.shape
    width = d // 2
    in_w = w_in.shape[-1]
    row = lambda i: (i, 0)
    lay3 = lambda i: (layer, 0, 0)
    kv_transposed = w_kvt is not None
    f32_out = jax.ShapeDtypeStruct((n, width), F32)
    bf_out = jax.ShapeDtypeStruct((n, width), BF16)
    out_spec = pl.BlockSpec((tm, width), row)
    if kv_transposed:
        _, batch, _, t_len = k_stack.shape
        tpb = t_len // tm
        extra, extra_specs = [w_kvt], [_resident((None, 2 * width, d), lay3)]
        stacks = [k_stack, v_stack]
        stack_spec = pl.BlockSpec((None, None, width, tm), lambda i: (layer, i // tpb, 0, i % tpb))
        kb_shape = jax.ShapeDtypeStruct((batch, width, t_len), BF16)
        kb_spec = pl.BlockSpec((None, width, tm), lambda i: (i // tpb, 0, i % tpb))
        va_shape = jax.ShapeDtypeStruct((1, n, width), F32)
        va_spec = pl.BlockSpec((None, tm, width), lambda i: (0, i, 0))
        aliases = {5: 5, 6: 6}
    else:
        extra, extra_specs = [], []
        stacks = [k_stack, v_stack, va_stack]
        stack_spec = pl.BlockSpec((None, tm, width), lambda i: (layer, i, 0))
        kb_shape, kb_spec = bf_out, out_spec
        va_shape, va_spec = jax.ShapeDtypeStruct(va_stack.shape, F32), stack_spec
        aliases = {4: 5, 5: 6, 6: 1}
    return pl.pallas_call(
        functools.partial(_in_proj_body, width=width, q_scale=(width // SB_HEADS) ** -0.5, n_stacks=len(stacks),
                          kv_transposed=kv_transposed),
        grid=(n // tm,),
        in_specs=[pl.BlockSpec((tm, d), row),
                  _resident((None, 1, d), lay3),
                  _resident((None, d, in_w), lay3),
                  _resident((None, SGU_GROUPS, width // SGU_GROUPS), lay3)]
                 + extra_specs + [pl.BlockSpec(memory_space=pl.ANY)] * len(stacks),
        out_specs=[out_spec, va_spec, out_spec, kb_spec, kb_spec, stack_spec, stack_spec],
        out_shape=[f32_out, va_shape, bf_out, kb_shape, kb_shape,
                   jax.ShapeDtypeStruct(k_stack.shape, F32), jax.ShapeDtypeStruct(v_stack.shape, F32)],
        input_output_aliases=aliases,
        compiler_params=_params(1),
        name="in_proj",
    )(x, ln, w_in, g_v, *extra, *stacks)


def _neg_log_keep(z):
    return jnp.maximum(z, 0.0) + jnp.log(1.0 + jnp.exp2(jnp.abs(z) * (-LOG2E)))


def _neg_suffix_matrix(n):
    r = lax.broadcasted_iota(jnp.int32, (n, n), 0)
    c = lax.broadcasted_iota(jnp.int32, (n, n), 1)
    return jnp.where(r >= c, -1.0, 0.0).astype(BF16)


def _sb_prompt_body(q_ref, kt_ref, vt_ref, o_ref, nuo_ref, qm_ref, z_ref, l_ref, a_ref, carry_ref, acc_ref, *, dh):
    t_len, gw = q_ref.shape
    tile = SB_TILE
    n_heads = gw // dh
    rows = n_heads * tile
    part_rows = rows // SB_SPLIT
    slabs = part_rows // LANES
    nuo_ref[...] = _neg_suffix_matrix(tile)
    lane_head = lax.broadcasted_iota(jnp.int32, (tile, gw), 1) // dh

    def stage_logits(c, masked, slot, part):
        k0 = pl.multiple_of(c * tile, tile)
        base = part * part_rows
        return jnp.dot(qm_ref[base:base + part_rows, :], kt_ref[:, pl.ds(k0, tile)], preferred_element_type=F32)

    def stage_keep(c, masked, slot, part, z_all):
        base = part * part_rows
        for sl in range(slabs):
            r0 = base + sl * LANES
            z = z_all[sl * LANES:(sl + 1) * LANES]
            nlk = _neg_log_keep(z)
            if masked:
                qpos = (r0 % tile) + lax.broadcasted_iota(jnp.int32, (LANES, tile), 0)
                kpos = lax.broadcasted_iota(jnp.int32, (LANES, tile), 1)
                nlk = jnp.where(kpos < qpos, nlk, 0.0)
            z_ref[slot, r0:r0 + LANES, :] = z
            l_ref[slot, r0:r0 + LANES, :] = nlk.astype(BF16)
        return jnp.dot(l_ref[slot, base:base + part_rows, :], nuo_ref[...], preferred_element_type=F32)

    def stage_weights(c, masked, slot, part, r_all):
        k0 = pl.multiple_of(c * tile, tile)
        base = part * part_rows
        for sl in range(slabs):
            r0 = base + sl * LANES
            r = r_all[sl * LANES:(sl + 1) * LANES]
            total = jnp.broadcast_to(r[:, 0:1], (LANES, LANES))
            if masked:
                cum = r
                carry_ref[r0:r0 + LANES, :] = total
            else:
                carry = carry_ref[r0:r0 + LANES, :]
                cum = r + jnp.concatenate([carry] * (tile // LANES), axis=1)
                carry_ref[r0:r0 + LANES, :] = carry + total
            a = jnp.exp(z_ref[slot, r0:r0 + LANES, :] + cum)
            if masked:
                qpos = (r0 % tile) + lax.broadcasted_iota(jnp.int32, (LANES, tile), 0)
                kpos = lax.broadcasted_iota(jnp.int32, (LANES, tile), 1)
                a = jnp.where(kpos < qpos, a, 0.0)
            a_ref[slot, r0:r0 + LANES, :] = a.astype(BF16)
        o_part = lax.dot_general(a_ref[slot, base:base + part_rows, :], vt_ref[:, pl.ds(k0, tile)],
                                 (((1,), (1,)), ((), ())), preferred_element_type=F32)
        if masked:
            acc_ref[base:base + part_rows, :] = o_part
        else:
            acc_ref[base:base + part_rows, :] += o_part

    def fold(*tiles):
        items = [(c, masked, slot, part) for part in range(SB_SPLIT) for (c, masked, slot) in tiles]
        logits, sums = {}, {}
        for step in range(len(items) + 3):
            if step < len(items):
                logits[step] = stage_logits(*items[step])
            if 0 <= step - 1 < len(items):
                sums[step - 1] = stage_keep(*items[step - 1], logits.pop(step - 1))
            if 0 <= step - 3 < len(items):
                stage_weights(*items[step - 3], sums.pop(step - 3))

    def begin(q0):
        q = q_ref[pl.ds(q0, tile), :].astype(F32)
        for h in range(n_heads):
            qm_ref[h * tile:(h + 1) * tile, :] = jnp.where(lane_head == h, q, 0.0).astype(BF16)

    def finish(q0):
        out = acc_ref[0:tile, :]
        for h in range(1, n_heads):
            out = jnp.where(lane_head == h, acc_ref[h * tile:(h + 1) * tile, :], out)
        o_ref[pl.ds(q0, tile), :] = out

    begin(0)
    fold((0, True, 0))
    finish(0)

    def q_tile(j, _):
        q0 = pl.multiple_of(j * tile, tile)
        begin(q0)
        fold((j, True, 0), (j - 1, False, 1))

        def more(state):
            c, max_carry = state
            return (c >= 0) & (max_carry >= SB_EXIT_LOG)

        def earlier(state):
            c, _ = state
            fold((c, False, 0))
            return c - 1, jnp.max(carry_ref[...])

        lax.while_loop(more, earlier, (j - 2, jnp.max(carry_ref[...])))
        finish(q0)
        return 0

    lax.fori_loop(1, t_len // tile, q_tile, 0)


def _sb_prompt(qs, ktb, vtb, batch, t_len):
    n, width = qs.shape
    dh = width // SB_HEADS
    gw = SB_GROUP * dh
    groups = width // gw
    rows = SB_GROUP * SB_TILE
    blk = pl.BlockSpec((t_len, gw), lambda b, g: (b, g))
    blk_t = pl.BlockSpec((None, gw, t_len), lambda b, g: (b, g, 0))
    return pl.pallas_call(
        functools.partial(_sb_prompt_body, dh=dh),
        grid=(batch, groups),
        in_specs=[blk, blk_t, blk_t],
        out_specs=blk,
        out_shape=jax.ShapeDtypeStruct((n, width), F32),
        scratch_shapes=[pltpu.VMEM((SB_TILE, SB_TILE), BF16),
                        pltpu.VMEM((rows, gw), BF16),
                        pltpu.VMEM((2, rows, SB_TILE), F32),
                        pltpu.VMEM((2, rows, SB_TILE), BF16),
                        pltpu.VMEM((2, rows, SB_TILE), BF16),
                        pltpu.VMEM((rows, LANES), F32),
                        pltpu.VMEM((rows, gw), F32)],
        compiler_params=_params(2),
        name="sb_prompt",
    )(qs, ktb, vtb)


def _sb_sample_body(q_ref, kc_ref, vc_ref, kn_ref, vn_ref, o_ref, nsm_ref, carry_ref, acc_ref, *, past, n_new):
    n_streams, width, _ = kc_ref.shape
    dh = width // SB_HEADS
    rows = SB_HEADS * n_new
    tile = SB_TILE
    nsm_ref[...] = _neg_suffix_matrix(tile)
    own = (lax.broadcasted_iota(jnp.int32, (rows, width), 0) // n_new
           == lax.broadcasted_iota(jnp.int32, (rows, width), 1) // dh)

    def stream_rows(ref, s):
        return ref[s * n_new:(s + 1) * n_new, :]

    q_bd = [jnp.where(own, jnp.concatenate([stream_rows(q_ref, s).astype(F32)] * SB_HEADS, axis=0), 0.0).astype(BF16)
            for s in range(n_streams)]

    def fold(s, kt_blk, vt_blk, valid):
        nk = kt_blk.shape[1]
        z = jnp.dot(q_bd[s], kt_blk, preferred_element_type=F32)
        nlk = _neg_log_keep(z)
        if valid is not None:
            nlk = jnp.where(valid, nlk, 0.0)
        r = jnp.dot(nlk.astype(BF16), nsm_ref[0:nk, 0:nk], preferred_element_type=F32)
        total = jnp.broadcast_to(r[:, 0:1], (rows, LANES))
        if valid is not None:
            a = jnp.where(valid, jnp.exp(z + r), 0.0)
            carry_ref[s] = total
        else:
            carry = carry_ref[s]
            a = jnp.exp(z + r + jnp.concatenate([carry] * (nk // LANES), axis=1))
            carry_ref[s] = carry + total
        o_blk = lax.dot_general(a.astype(BF16), vt_blk, (((1,), (1,)), ((), ())), preferred_element_type=F32)
        if valid is not None:
            acc_ref[s] = o_blk
        else:
            acc_ref[s] += o_blk

    zeros = jnp.zeros((width, LANES - n_new), F32)
    valid_new = (lax.broadcasted_iota(jnp.int32, (rows, LANES), 1)
                 < lax.broadcasted_iota(jnp.int32, (rows, LANES), 0) % n_new)
    for s in range(n_streams):
        kt_new = jnp.concatenate([stream_rows(kn_ref, s).astype(F32).T, zeros], axis=1).astype(BF16)
        vt_new = jnp.concatenate([stream_rows(vn_ref, s).astype(F32).T, zeros], axis=1).astype(BF16)
        fold(s, kt_new, vt_new, valid_new)

    def more(state):
        c, max_carry = state
        return (c >= 0) & (max_carry >= SB_EXIT_LOG)

    def earlier(state):
        c, _ = state
        k0 = pl.multiple_of(c * tile, tile)
        for s in range(n_streams):
            fold(s, kc_ref[s, :, pl.ds(k0, tile)].astype(BF16), vc_ref[s, :, pl.ds(k0, tile)].astype(BF16), None)
        return c - 1, jnp.max(carry_ref[...])

    lax.while_loop(more, earlier, (past // tile - 1, jnp.max(carry_ref[...])))
    for s in range(n_streams):
        o_all = jnp.where(own, acc_ref[s], 0.0)
        out = o_all[0:n_new, :]
        for h in range(1, SB_HEADS):
            out = out + o_all[h * n_new:(h + 1) * n_new, :]
        o_ref[s * n_new:(s + 1) * n_new, :] = out


def _sb_sample(qs, kb, vb, cache_kt, cache_vt, layer, n_new):
    n, width = qs.shape
    bs, past = cache_kt.shape[1], cache_kt.shape[3]
    assert past % SB_TILE == 0 and n_new % 8 == 0 and n_new <= LANES
    ns = SB_SAMPLE_STREAMS if bs % SB_SAMPLE_STREAMS == 0 else 1
    rows = SB_HEADS * n_new
    row = pl.BlockSpec((ns * n_new, width), lambda b: (b, 0))
    cache = pl.BlockSpec((None, ns, width, past), lambda b: (layer, b, 0, 0))
    return pl.pallas_call(
        functools.partial(_sb_sample_body, past=past, n_new=n_new),
        grid=(bs // ns,),
        in_specs=[row, cache, cache, row, row],
        out_specs=row,
        out_shape=jax.ShapeDtypeStruct((n, width), F32),
        scratch_shapes=[pltpu.VMEM((SB_TILE, SB_TILE), BF16),
                        pltpu.VMEM((ns, rows, LANES), F32),
                        pltpu.VMEM((ns, rows, width), F32)],
        compiler_params=_params(1),
        name="sb_sample",
    )(qs, cache_kt, cache_vt, kb, vb)


def _merge_rows(x, u_ref, va_ref, sb_ref, ws_ref, bs_ref, ga_ref, gb_ref, wo_ref, cat_ref):
    tm = x.shape[0]
    width = u_ref.shape[1]
    e = width // SGU_GROUPS
    r = lax.broadcasted_iota(jnp.int32, (SGU_CHUNK, SGU_CHUNK), 0)
    c = lax.broadcasted_iota(jnp.int32, (SGU_CHUNK, SGU_CHUNK), 1)
    n_chunks = tm // SGU_CHUNK
    cat_ref[:, width:] = _rmsnorm(sb_ref[...], gb_ref[...]).astype(BF16)
    va = va_ref[...].astype(BF16)
    mixes = []
    for g in range(SGU_GROUPS):
        w_g = jnp.where(r >= c, ws_ref[g], 0.0).astype(BF16)
        chunks = jnp.concatenate([va[ch * SGU_CHUNK:(ch + 1) * SGU_CHUNK, g * e:(g + 1) * e]
                                  for ch in range(n_chunks)], axis=1)
        mixes.append(jnp.dot(w_g, chunks, preferred_element_type=F32) + bs_ref[:, g:g + 1])
    mix = jnp.concatenate([jnp.concatenate([m[:, ch * e:(ch + 1) * e] for m in mixes], axis=1)
                           for ch in range(n_chunks)], axis=0)
    cat_ref[:, 0:width] = _rmsnorm(u_ref[...] * mix, ga_ref[...]).astype(BF16)
    return x + jnp.dot(cat_ref[...], wo_ref[...], preferred_element_type=F32)


def _merge_body(x_ref, u_ref, va_ref, sb_ref, ws_ref, bs_ref, ga_ref, gb_ref, wo_ref, o_ref, cat_ref):
    o_ref[...] = _merge_rows(x_ref[...], u_ref, va_ref, sb_ref, ws_ref, bs_ref, ga_ref, gb_ref, wo_ref, cat_ref)


def _merge_specs(u, sb, d, va_layer, layer, tm):
    width, sb_width = u.shape[1], sb.shape[1]
    row = lambda i: (i, 0)
    lay3 = lambda i: (layer, 0, 0)
    return [pl.BlockSpec((tm, d), row),
            pl.BlockSpec((tm, width), row),
            pl.BlockSpec((None, tm, width), lambda i: (va_layer, i, 0)),
            pl.BlockSpec((tm, sb_width), row),
            _resident((SGU_GROUPS, SGU_CHUNK, SGU_CHUNK), lambda i: (0, 0, 0)),
            _resident((SGU_CHUNK, SGU_GROUPS), lambda i: (0, 0)),
            _resident((None, 1, width), lay3),
            _resident((None, 1, sb_width), lay3),
            _resident((None, width + sb_width, d), lay3)]


def _merge(x, u, va, va_layer, sb, w_mix, b_mix, g_a, g_b, w_out, layer, tm):
    n, d = x.shape
    return pl.pallas_call(
        _merge_body,
        grid=(n // tm,),
        in_specs=_merge_specs(u, sb, d, va_layer, layer, tm),
        out_specs=pl.BlockSpec((tm, d), lambda i: (i, 0)),
        out_shape=jax.ShapeDtypeStruct((n, d), F32),
        scratch_shapes=[pltpu.VMEM((tm, u.shape[1] + sb.shape[1]), BF16)],
        compiler_params=_params(1),
        name="merge",
    )(x, u, va, sb, w_mix, b_mix, g_a, g_b, w_out)


def _mem_kv_body(m_ref, wk_ref, wv_ref, k_ref, v_ref):
    m = m_ref[...].astype(BF16)
    k_ref[...] = jnp.dot(m, wk_ref[...], preferred_element_type=F32)
    v_ref[...] = jnp.dot(m, wv_ref[...], preferred_element_type=F32)


def _mem_kv(mem, w_mk, w_mv):
    b, m, d = mem.shape
    depth = w_mk.shape[0]
    out = jax.ShapeDtypeStruct((depth, b, m, d), F32)
    w_spec = pl.BlockSpec((None, d, d), lambda l, i: (l, 0, 0))
    o_spec = pl.BlockSpec((None, None, m, d), lambda l, i: (l, i, 0, 0))
    return pl.pallas_call(
        _mem_kv_body,
        grid=(depth, b),
        in_specs=[pl.BlockSpec((None, m, d), lambda l, i: (i, 0, 0)), w_spec, w_spec],
        out_specs=[o_spec, o_spec],
        out_shape=[out, out],
        compiler_params=_params(2),
        name="mem_kv",
    )(mem, w_mk, w_mv)


def _mem_attend_rows(x, ln_ref, wq_ref, mk_ref, mv_ref, wo_ref, oc_ref, n_streams):
    tm, d = x.shape
    dh = d // MEM_HEADS
    rpb = tm // n_streams
    h = _rmsnorm(x, ln_ref[...]).astype(BF16)
    q = (jnp.dot(h, wq_ref[...], preferred_element_type=F32) * dh ** -0.5)
    nt_dims = (((1,), (1,)), ((), ()))
    if MEM_HEADS * rpb <= LANES:
        stacked = MEM_HEADS * rpb
        own = (lax.broadcasted_iota(jnp.int32, (stacked, d), 0) // rpb
               == lax.broadcasted_iota(jnp.int32, (stacked, d), 1) // dh)
        for s in range(n_streams):
            rows = slice(s * rpb, (s + 1) * rpb)
            q_bd = jnp.where(own, jnp.concatenate([q[rows]] * MEM_HEADS, axis=0), 0.0).astype(BF16)
            k_s = jnp.concatenate([mk_ref[s, hd] for hd in range(MEM_HEADS)], axis=1).astype(BF16)
            v_s = jnp.concatenate([mv_ref[s, hd] for hd in range(MEM_HEADS)], axis=1).astype(BF16)
            sc = lax.dot_general(q_bd, k_s, nt_dims, preferred_element_type=F32)
            p = jnp.exp(sc - jnp.max(sc, axis=-1, keepdims=True))
            o = jnp.dot(p.astype(BF16), v_s, preferred_element_type=F32)
            o = jnp.where(own, o / jnp.sum(p, axis=-1, keepdims=True), 0.0)
            o_s = o[0:rpb]
            for hd in range(1, MEM_HEADS):
                o_s = o_s + o[hd * rpb:(hd + 1) * rpb]
            oc_ref[rows, :] = o_s.astype(BF16)
    else:
        q = q.astype(BF16)
        for s in range(n_streams):
            rows = slice(s * rpb, (s + 1) * rpb)
            for hd in range(MEM_HEADS):
                cols = slice(hd * dh, (hd + 1) * dh)
                k = mk_ref[s, :, cols].astype(BF16)
                v = mv_ref[s, :, cols].astype(BF16)
                sc = lax.dot_general(q[rows, cols], k, nt_dims, preferred_element_type=F32)
                p = jnp.exp(sc - jnp.max(sc, axis=-1, keepdims=True))
                o = jnp.dot(p.astype(BF16), v, preferred_element_type=F32) / jnp.sum(p, axis=-1, keepdims=True)
                oc_ref[rows, cols] = o.astype(BF16)
    return x + jnp.dot(oc_ref[...], wo_ref[...], preferred_element_type=F32)


def _mem_attn_body(x_ref, ln_ref, wq_ref, mk_ref, mv_ref, wo_ref, o_ref, oc_ref, *, n_streams):
    o_ref[...] = _mem_attend_rows(x_ref[...], ln_ref, wq_ref, mk_ref, mv_ref, wo_ref, oc_ref, n_streams)


def _mem_specs(mk, d, layer, tm, rows_per_stream):
    n_streams = max(1, tm // rows_per_stream)
    tiles_per_stream = max(1, rows_per_stream // tm)
    lay3 = lambda i: (layer, 0, 0)
    if mk.ndim == 5:
        mem_spec = pl.BlockSpec((None, n_streams) + mk.shape[2:], lambda i: (layer, i // tiles_per_stream, 0, 0, 0))
    else:
        mem_spec = pl.BlockSpec((None, n_streams) + mk.shape[2:], lambda i: (layer, i // tiles_per_stream, 0, 0))
    return n_streams, [_resident((None, 1, d), lay3), _resident((None, d, d), lay3), mem_spec, mem_spec,
                       _resident((None, d, d), lay3)]


def _mem_attn(x, ln, w_mq, mk, mv, w_mo, layer, tm, rows_per_stream):
    n, d = x.shape
    row = lambda i: (i, 0)
    n_streams, specs = _mem_specs(mk, d, layer, tm, rows_per_stream)
    return pl.pallas_call(
        functools.partial(_mem_attn_body, n_streams=n_streams),
        grid=(n // tm,),
        in_specs=[pl.BlockSpec((tm, d), row)] + specs,
        out_specs=pl.BlockSpec((tm, d), row),
        out_shape=jax.ShapeDtypeStruct((n, d), F32),
        scratch_shapes=[pltpu.VMEM((tm, d), BF16)],
        compiler_params=_params(1),
        name="mem_attn",
    )(x, ln, w_mq, mk, mv, w_mo)


def _merge_mem_body(x_ref, u_ref, va_ref, sb_ref, ws_ref, bs_ref, ga_ref, gb_ref, wo_ref,
                    ln_ref, wq_ref, mk_ref, mv_ref, wmo_ref, o_ref, cat_ref, oc_ref, *, n_streams):
    x = _merge_rows(x_ref[...], u_ref, va_ref, sb_ref, ws_ref, bs_ref, ga_ref, gb_ref, wo_ref, cat_ref)
    o_ref[...] = _mem_attend_rows(x, ln_ref, wq_ref, mk_ref, mv_ref, wmo_ref, oc_ref, n_streams)


def _merge_mem(x, u, va, va_layer, sb, w_mix, b_mix, g_a, g_b, w_out, ln, w_mq, mk, mv, w_mo, layer, tm,
               rows_per_stream):
    n, d = x.shape
    n_streams, mem_specs = _mem_specs(mk, d, layer, tm, rows_per_stream)
    return pl.pallas_call(
        functools.partial(_merge_mem_body, n_streams=n_streams),
        grid=(n // tm,),
        in_specs=_merge_specs(u, sb, d, va_layer, layer, tm) + mem_specs,
        out_specs=pl.BlockSpec((tm, d), lambda i: (i, 0)),
        out_shape=jax.ShapeDtypeStruct((n, d), F32),
        scratch_shapes=[pltpu.VMEM((tm, u.shape[1] + sb.shape[1]), BF16), pltpu.VMEM((tm, d), BF16)],
        compiler_params=_params(1),
        name="merge_mem",
    )(x, u, va, sb, w_mix, b_mix, g_a, g_b, w_out, ln, w_mq, mk, mv, w_mo)


def _ffn_body(x_ref, ln_ref, wg_ref, wu_ref, wd_ref, lnf_ref, o_ref, *, final_norm):
    x = x_ref[...]
    h = _rmsnorm(x, ln_ref[...]).astype(BF16)
    gate = jnp.dot(h, wg_ref[...], preferred_element_type=F32)
    up = jnp.dot(h, wu_ref[...], preferred_element_type=F32)
    act = (jax.nn.silu(gate) * up).astype(BF16)
    y = x + jnp.dot(act, wd_ref[...], preferred_element_type=F32)
    if final_norm:
        y = _rmsnorm(y, lnf_ref[...])
    o_ref[...] = y


def _ffn(x, ln, w_gate, w_up, w_down, ln_final, layer, tm, final_norm):
    n, d = x.shape
    d_ff = w_gate.shape[-1]
    row = lambda i: (i, 0)
    lay3 = lambda i: (layer, 0, 0)
    return pl.pallas_call(
        functools.partial(_ffn_body, final_norm=final_norm),
        grid=(n // tm,),
        in_specs=[pl.BlockSpec((tm, d), row),
                  _resident((None, 1, d), lay3),
                  _resident((None, d, d_ff), lay3),
                  _resident((None, d, d_ff), lay3),
                  _resident((None, d_ff, d), lay3),
                  _resident((1, d), lambda i: (0, 0))],
        out_specs=pl.BlockSpec((tm, d), row),
        out_shape=jax.ShapeDtypeStruct((n, d), F32),
        compiler_params=_params(1),
        name="ffn",
    )(x, ln, w_gate, w_up, w_down, ln_final)


def kernel(x_prompt, x_sample, cache_sb_k, cache_sb_v, cache_mem_k, cache_mem_v, mem_prompt, ln_mix, w_in, g_sgu_v, w_sgu, b_sgu, g_out_sgu, g_out_sb, w_out, ln_mem, w_mq, w_mk, w_mv, w_mo, ln_ffn, w_ffn_gate, w_ffn_up, w_ffn_down, ln_final):
    batch, t_len, d = x_prompt.shape
    bs, n_new, _ = x_sample.shape
    depth = w_in.shape[0]
    past = cache_sb_k.shape[2]
    n_mem = mem_prompt.shape[1]
    width = d // 2
    dh = width // SB_HEADS
    e = width // SGU_GROUPS
    assert t_len % ROW_TILE == 0 and t_len % SB_TILE == 0 and SGU_CHUNK % n_new == 0

    w_in_b, w_out_b = w_in.astype(BF16), w_out.astype(BF16)
    w_kvt_b = jnp.swapaxes(w_in_b[:, :, 3 * width:], 1, 2)
    w_mq_b, w_mk_b, w_mv_b, w_mo_b = (w.astype(BF16) for w in (w_mq, w_mk, w_mv, w_mo))
    w_g_b, w_u_b, w_d_b = (w.astype(BF16) for w in (w_ffn_gate, w_ffn_up, w_ffn_down))
    as_rows = lambda g: g.reshape(depth, 1, -1)
    ln_mix_r, ln_mem_r, ln_ffn_r = as_rows(ln_mix), as_rows(ln_mem), as_rows(ln_ffn)
    g_a_r, g_b_r = as_rows(g_out_sgu), as_rows(g_out_sb)
    ln_final_r = ln_final.reshape(1, d)

    cache_kt = cache_sb_k.transpose(0, 1, 3, 4, 2).reshape(depth, bs, width, past)
    cache_vt = cache_sb_v.transpose(0, 1, 3, 4, 2).reshape(depth, bs, width, past)
    cmem_k = cache_mem_k.transpose(0, 1, 3, 2, 4)
    cmem_v = cache_mem_v.transpose(0, 1, 3, 2, 4)

    mk_p, mv_p = _mem_kv(mem_prompt, w_mk_b, w_mv_b)

    n_p = batch * t_len
    x = x_prompt.reshape(n_p, d)
    kt_st = jnp.zeros((depth, batch, width, t_len), F32)
    vt_st = jnp.zeros((depth, batch, width, t_len), F32)
    for l in range(depth):
        u, va, qs, ktb, vtb, kt_st, vt_st = _in_proj(x, ln_mix_r, w_in_b, w_kvt_b, g_sgu_v, kt_st, vt_st, None, l, ROW_TILE_WIDE)
        sb = _sb_prompt(qs, ktb, vtb, batch, t_len)
        x = _merge_mem(x, u, va, 0, sb, w_sgu[l], b_sgu[l].T, g_a_r, g_b_r, w_out_b,
                       ln_mem_r, w_mq_b, mk_p, mv_p, w_mo_b, l, ROW_TILE, t_len)
        x = _ffn(x, ln_ffn_r, w_g_b, w_u_b, w_d_b, ln_final_r, l, ROW_TILE, l == depth - 1)
    y_prompt = x.reshape(batch, t_len, d)
    sb_k_prompt = kt_st.reshape(depth, batch, SB_HEADS, dh, t_len).transpose(0, 1, 4, 2, 3)
    sb_v_prompt = vt_st.reshape(depth, batch, SB_HEADS, dh, t_len).transpose(0, 1, 4, 2, 3)

    n_s = bs * n_new
    tm_s = min(ROW_TILE, n_s)
    tm_mem = 8 * n_new
    reps = SGU_CHUNK // n_new
    x = x_sample.reshape(n_s, d)
    k_st = jnp.zeros((depth, n_s, width), F32)
    v_st = jnp.zeros((depth, n_s, width), F32)
    va_st = jnp.zeros((depth, n_s, width), F32)
    for l in range(depth):
        u, va_st, qs, kb, vb, k_st, v_st = _in_proj(x, ln_mix_r, w_in_b, None, g_sgu_v, k_st, v_st, va_st, l, tm_s)
        sb = _sb_sample(qs, kb, vb, cache_kt, cache_vt, l, n_new)
        w_blk = jnp.einsum('ab,gts->gatbs', jnp.eye(reps, dtype=F32), w_sgu[l][:, :n_new, :n_new])
        w_blk = w_blk.reshape(SGU_GROUPS, SGU_CHUNK, SGU_CHUNK)
        b_blk = jnp.tile(b_sgu[l][:, :n_new], (1, reps)).T
        x = _merge(x, u, va_st, l, sb, w_blk, b_blk, g_a_r, g_b_r, w_out_b, l, tm_s)
        x = _mem_attn(x, ln_mem_r, w_mq_b, cmem_k, cmem_v, w_mo_b, l, tm_mem, n_new)
        x = _ffn(x, ln_ffn_r, w_g_b, w_u_b, w_d_b, ln_final_r, l, tm_s, l == depth - 1)
    y_sample = x.reshape(bs, n_new, d)

    return (y_prompt, y_sample, sb_k_prompt, sb_v_prompt,
            mk_p.reshape(depth, batch, n_mem, MEM_HEADS, d // MEM_HEADS),
            mv_p.reshape(depth, batch, n_mem, MEM_HEADS, d // MEM_HEADS),
            k_st.reshape(depth, bs, n_new, SB_HEADS, dh),
            v_st.reshape(depth, bs, n_new, SB_HEADS, dh),
            va_st.reshape(depth, bs, n_new, SGU_GROUPS, e))
```

```python
import functools

import jax
import jax.numpy as jnp
from jax import lax
from jax.experimental import pallas as pl
from jax.experimental.pallas import tpu as pltpu

F32 = jnp.float32
BF16 = jnp.bfloat16
EPS = 1e-6

LANES = 128
SGU_CHUNK = 128
SGU_GROUPS = 4
SB_HEADS = 8
MEM_HEADS = 4
ROW_TILE = 512
ROW_TILE_WIDE = 1024
SB_TILE = 256
SB_GROUP = 4
SB_SPLIT = 4
LOG2E = 1.4426950408889634
SB_SAMPLE_STREAMS = 2
SB_EXIT_LOG = -106.0
VMEM_LIMIT = 56 * 1024 * 1024


def _params(n_axes):
    return pltpu.CompilerParams(dimension_semantics=("parallel",) * n_axes,
                                vmem_limit_bytes=VMEM_LIMIT)


def _resident(block_shape, index_map):
    return pl.BlockSpec(block_shape, index_map, pipeline_mode=pl.Buffered(1))


def _rmsnorm(x, g):
    return x * lax.rsqrt(jnp.mean(x * x, axis=-1, keepdims=True) + EPS) * g


def _in_proj_body(x_ref, ln_ref, w_ref, gv_ref, *refs, width, q_scale, n_stacks, kv_transposed):
    n_extra = 1 if kv_transposed else 0
    u_ref, va_ref, qs_ref, kb_ref, vb_ref, k_ref, v_ref = refs[n_extra + n_stacks:n_extra + n_stacks + 7]
    h = _rmsnorm(x_ref[...], ln_ref[...]).astype(BF16)

    def proj(j):
        return jnp.dot(h, w_ref[:, j * width:(j + 1) * width], preferred_element_type=F32)

    u_ref[...] = jax.nn.gelu(proj(0))
    va = jax.nn.gelu(proj(1))
    e = width // SGU_GROUPS
    for g in range(SGU_GROUPS):
        va_ref[:, g * e:(g + 1) * e] = _rmsnorm(va[:, g * e:(g + 1) * e], gv_ref[g:g + 1, :])
    qs_ref[...] = (proj(2) * q_scale).astype(BF16)
    if kv_transposed:
        kvt = lax.dot_general(refs[0][...], h, (((1,), (1,)), ((), ())), preferred_element_type=F32)
        k, v = kvt[:width], kvt[width:]
    else:
        k, v = proj(3), proj(4)
    k_ref[...] = k
    kb_ref[...] = k.astype(BF16)
    v_ref[...] = v
    vb_ref[...] = v.astype(BF16)


def _in_proj(x, ln, w_in, w_kvt, g_v, k_stack, v_stack, va_stack, layer, tm):
    n, d = x.shape
    width = d // 2
    in_w = w_in.shape[-1]
    row = lambda i: (i, 0)
    lay3 = lambda i: (layer, 0, 0)
    kv_transposed = w_kvt is not None
    f32_out = jax.ShapeDtypeStruct((n, width), F32)
    bf_out = jax.ShapeDtypeStruct((n, width), BF16)
    out_spec = pl.BlockSpec((tm, width), row)
    if kv_transposed:
        _, batch, _, t_len = k_stack.shape
        tpb = t_len // tm
        extra, extra_specs = [w_kvt], [_resident((None, 2 * width, d), lay3)]
        stacks = [k_stack, v_stack]
        stack_spec = pl.BlockSpec((None, None, width, tm), lambda i: (layer, i // tpb, 0, i % tpb))
        kb_shape = jax.ShapeDtypeStruct((batch, width, t_len), BF16)
        kb_spec = pl.BlockSpec((None, width, tm), lambda i: (i // tpb, 0, i % tpb))
        va_shape = jax.ShapeDtypeStruct((1, n, width), F32)
        va_spec = pl.BlockSpec((None, tm, width), lambda i: (0, i, 0))
        aliases = {5: 5, 6: 6}
    else:
        extra, extra_specs = [], []
        stacks = [k_stack, v_stack, va_stack]
        stack_spec = pl.BlockSpec((None, tm, width), lambda i: (layer, i, 0))
        kb_shape, kb_spec = bf_out, out_spec
        va_shape, va_spec = jax.ShapeDtypeStruct(va_stack.shape, F32), stack_spec
        aliases = {4: 5, 5: 6, 6: 1}
    return pl.pallas_call(
        functools.partial(_in_proj_body, width=width, q_scale=(width // SB_HEADS) ** -0.5, n_stacks=len(stacks),
                          kv_transposed=kv_transposed),
        grid=(n // tm,),
        in_specs=[pl.BlockSpec((tm, d), row),
                  _resident((None, 1, d), lay3),
                  _resident((None, d, in_w), lay3),
                  _resident((None, SGU_GROUPS, width // SGU_GROUPS), lay3)]
                 + extra_specs + [pl.BlockSpec(memory_space=pl.ANY)] * len(stacks),
        out_specs=[out_spec, va_spec, out_spec, kb_spec, kb_spec, stack_spec, stack_spec],
        out_shape=[f32_out, va_shape, bf_out, kb_shape, kb_shape,
                   jax.ShapeDtypeStruct(k_stack.shape, F32), jax.ShapeDtypeStruct(v_stack.shape, F32)],
        input_output_aliases=aliases,
        compiler_params=_params(1),
        name="in_proj",
    )(x, ln, w_in, g_v, *extra, *stacks)


def _neg_log_keep(z):
    return jnp.maximum(z, 0.0) + jnp.log(1.0 + jnp.exp2(jnp.abs(z) * (-LOG2E)))


def _neg_suffix_matrix(n):
    r = lax.broadcasted_iota(jnp.int32, (n, n), 0)
    c = lax.broadcasted_iota(jnp.int32, (n, n), 1)
    return jnp.where(r >= c, -1.0, 0.0).astype(BF16)


def _sb_prompt_body(q_ref, kt_ref, vt_ref, o_ref, nuo_ref, qm_ref, z_ref, l_ref, a_ref, carry_ref, acc_ref, *, dh):
    t_len, gw = q_ref.shape
    tile = SB_TILE
    n_heads = gw // dh
    rows = n_heads * tile
    part_rows = rows // SB_SPLIT
    slabs = part_rows // LANES
    nuo_ref[...] = _neg_suffix_matrix(tile)
    lane_head = lax.broadcasted_iota(jnp.int32, (tile, gw), 1) // dh

    def stage_logits(c, masked, slot, part):
        k0 = pl.multiple_of(c * tile, tile)
        base = part * part_rows
        return jnp.dot(qm_ref[base:base + part_rows, :], kt_ref[:, pl.ds(k0, tile)], preferred_element_type=F32)

    def stage_keep(c, masked, slot, part, z_all):
        base = part * part_rows
        for sl in range(slabs):
            r0 = base + sl * LANES
            z = z_all[sl * LANES:(sl + 1) * LANES]
            nlk = _neg_log_keep(z)
            if masked:
                qpos = (r0 % tile) + lax.broadcasted_iota(jnp.int32, (LANES, tile), 0)
                kpos = lax.broadcasted_iota(jnp.int32, (LANES, tile), 1)
                nlk = jnp.where(kpos < qpos, nlk, 0.0)
            z_ref[slot, r0:r0 + LANES, :] = z
            l_ref[slot, r0:r0 + LANES, :] = nlk.astype(BF16)
        return jnp.dot(l_ref[slot, base:base + part_rows, :], nuo_ref[...], preferred_element_type=F32)

    def stage_weights(c, masked, slot, part, r_all):
        k0 = pl.multiple_of(c * tile, tile)
        base = part * part_rows
        for sl in range(slabs):
            r0 = base + sl * LANES
            r = r_all[sl * LANES:(sl + 1) * LANES]
            total = jnp.broadcast_to(r[:, 0:1], (LANES, LANES))
            if masked:
                cum = r
                carry_ref[r0:r0 + LANES, :] = total
            else:
                carry = carry_ref[r0:r0 + LANES, :]
                cum = r + jnp.concatenate([carry] * (tile // LANES), axis=1)
                carry_ref[r0:r0 + LANES, :] = carry + total
            a = jnp.exp(z_ref[slot, r0:r0 + LANES, :] + cum)
            if masked:
                qpos = (r0 % tile) + lax.broadcasted_iota(jnp.int32, (LANES, tile), 0)
                kpos = lax.broadcasted_iota(jnp.int32, (LANES, tile), 1)
                a = jnp.where(kpos < qpos, a, 0.0)
            a_ref[slot, r0:r0 + LANES, :] = a.astype(BF16)
        o_part = lax.dot_general(a_ref[slot, base:base + part_rows, :], vt_ref[:, pl.ds(k0, tile)],
                                 (((1,), (1,)), ((), ())), preferred_element_type=F32)
        if masked:
            acc_ref[base:base + part_rows, :] = o_part
        else:
            acc_ref[base:base + part_rows, :] += o_part

    def fold(*tiles):
        items = [(c, masked, slot, part) for part in range(SB_SPLIT) for (c, masked, slot) in tiles]
        logits, sums = {}, {}
        for step in range(len(items) + 3):
            if step < len(items):
                logits[step] = stage_logits(*items[step])
            if 0 <= step - 1 < len(items):
                sums[step - 1] = stage_keep(*items[step - 1], logits.pop(step - 1))
            if 0 <= step - 3 < len(items):
                stage_weights(*items[step - 3], sums.pop(step - 3))

    def begin(q0):
        q = q_ref[pl.ds(q0, tile), :].astype(F32)
        for h in range(n_heads):
            qm_ref[h * tile:(h + 1) * tile, :] = jnp.where(lane_head == h, q, 0.0).astype(BF16)

    def finish(q0):
        out = acc_ref[0:tile, :]
        for h in range(1, n_heads):
            out = jnp.where(lane_head == h, acc_ref[h * tile:(h + 1) * tile, :], out)
        o_ref[pl.ds(q0, tile), :] = out

    begin(0)
    fold((0, True, 0))
    finish(0)

    def q_tile(j, _):
        q0 = pl.multiple_of(j * tile, tile)
        begin(q0)
        fold((j, True, 0), (j - 1, False, 1))

        def more(state):
            c, max_carry = state
            return (c >= 0) & (max_carry >= SB_EXIT_LOG)

        def earlier(state):
            c, _ = state
            fold((c, False, 0))
            return c - 1, jnp.max(carry_ref[...])

        lax.while_loop(more, earlier, (j - 2, jnp.max(carry_ref[...])))
        finish(q0)
        return 0

    lax.fori_loop(1, t_len // tile, q_tile, 0)


def _sb_prompt(qs, ktb, vtb, batch, t_len):
    n, width = qs.shape
    dh = width // SB_HEADS
    gw = SB_GROUP * dh
    groups = width // gw
    rows = SB_GROUP * SB_TILE
    blk = pl.BlockSpec((t_len, gw), lambda b, g: (b, g))
    blk_t = pl.BlockSpec((None, gw, t_len), lambda b, g: (b, g, 0))
    return pl.pallas_call(
        functools.partial(_sb_prompt_body, dh=dh),
        grid=(batch, groups),
        in_specs=[blk, blk_t, blk_t],
        out_specs=blk,
        out_shape=jax.ShapeDtypeStruct((n, width), F32),
        scratch_shapes=[pltpu.VMEM((SB_TILE, SB_TILE), BF16),
                        pltpu.VMEM((rows, gw), BF16),
                        pltpu.VMEM((2, rows, SB_TILE), F32),
                        pltpu.VMEM((2, rows, SB_TILE), BF16),
                        pltpu.VMEM((2, rows, SB_TILE), BF16),
                        pltpu.VMEM((rows, LANES), F32),
                        pltpu.VMEM((rows, gw), F32)],
        compiler_params=_params(2),
        name="sb_prompt",
    )(qs, ktb, vtb)


def _sb_sample_body(q_ref, kc_ref, vc_ref, kn_ref, vn_ref, o_ref, nsm_ref, carry_ref, acc_ref, *, past, n_new):
    n_streams, width, _ = kc_ref.shape
    dh = width // SB_HEADS
    rows = SB_HEADS * n_new
    tile = SB_TILE
    nsm_ref[...] = _neg_suffix_matrix(tile)
    own = (lax.broadcasted_iota(jnp.int32, (rows, width), 0) // n_new
           == lax.broadcasted_iota(jnp.int32, (rows, width), 1) // dh)

    def stream_rows(ref, s):
        return ref[s * n_new:(s + 1) * n_new, :]

    q_bd = [jnp.where(own, jnp.concatenate([stream_rows(q_ref, s).astype(F32)] * SB_HEADS, axis=0), 0.0).astype(BF16)
            for s in range(n_streams)]

    def fold(s, kt_blk, vt_blk, valid):
        nk = kt_blk.shape[1]
        z = jnp.dot(q_bd[s], kt_blk, preferred_element_type=F32)
        nlk = _neg_log_keep(z)
        if valid is not None:
            nlk = jnp.where(valid, nlk, 0.0)
        r = jnp.dot(nlk.astype(BF16), nsm_ref[0:nk, 0:nk], preferred_element_type=F32)
        total = jnp.broadcast_to(r[:, 0:1], (rows, LANES))
        if valid is not None:
            a = jnp.where(valid, jnp.exp(z + r), 0.0)
            carry_ref[s] = total
        else:
            carry = carry_ref[s]
            a = jnp.exp(z + r + jnp.concatenate([carry] * (nk // LANES), axis=1))
            carry_ref[s] = carry + total
        o_blk = lax.dot_general(a.astype(BF16), vt_blk, (((1,), (1,)), ((), ())), preferred_element_type=F32)
        if valid is not None:
            acc_ref[s] = o_blk
        else:
            acc_ref[s] += o_blk

    zeros = jnp.zeros((width, LANES - n_new), F32)
    valid_new = (lax.broadcasted_iota(jnp.int32, (rows, LANES), 1)
                 < lax.broadcasted_iota(jnp.int32, (rows, LANES), 0) % n_new)
    for s in range(n_streams):
        kt_new = jnp.concatenate([stream_rows(kn_ref, s).astype(F32).T, zeros], axis=1).astype(BF16)
        vt_new = jnp.concatenate([stream_rows(vn_ref, s).astype(F32).T, zeros], axis=1).astype(BF16)
        fold(s, kt_new, vt_new, valid_new)

    def more(state):
        c, max_carry = state
        return (c >= 0) & (max_carry >= SB_EXIT_LOG)

    def earlier(state):
        c, _ = state
        k0 = pl.multiple_of(c * tile, tile)
        for s in range(n_streams):
            fold(s, kc_ref[s, :, pl.ds(k0, tile)].astype(BF16), vc_ref[s, :, pl.ds(k0, tile)].astype(BF16), None)
        return c - 1, jnp.max(carry_ref[...])

    lax.while_loop(more, earlier, (past // tile - 1, jnp.max(carry_ref[...])))
    for s in range(n_streams):
        o_all = jnp.where(own, acc_ref[s], 0.0)
        out = o_all[0:n_new, :]
        for h in range(1, SB_HEADS):
            out = out + o_all[h * n_new:(h + 1) * n_new, :]
        o_ref[s * n_new:(s + 1) * n_new, :] = out


def _sb_sample(qs, kb, vb, cache_kt, cache_vt, layer, n_new):
    n, width = qs.shape
    bs, past = cache_kt.shape[1], cache_kt.shape[3]
    assert past % SB_TILE == 0 and n_new % 8 == 0 and n_new <= LANES
    ns = SB_SAMPLE_STREAMS if bs % SB_SAMPLE_STREAMS == 0 else 1
    rows = SB_HEADS * n_new
    row = pl.BlockSpec((ns * n_new, width), lambda b: (b, 0))
    cache = pl.BlockSpec((None, ns, width, past), lambda b: (layer, b, 0, 0))
    return pl.pallas_call(
        functools.partial(_sb_sample_body, past=past, n_new=n_new),
        grid=(bs // ns,),
        in_specs=[row, cache, cache, row, row],
        out_specs=row,
        out_shape=jax.ShapeDtypeStruct((n, width), F32),
        scratch_shapes=[pltpu.VMEM((SB_TILE, SB_TILE), BF16),
                        pltpu.VMEM((ns, rows, LANES), F32),
                        pltpu.VMEM((ns, rows, width), F32)],
        compiler_params=_params(1),
        name="sb_sample",
    )(qs, cache_kt, cache_vt, kb, vb)


def _merge_rows(x, u_ref, va_ref, sb_ref, ws_ref, bs_ref, ga_ref, gb_ref, wo_ref, cat_ref):
    tm = x.shape[0]
    width = u_ref.shape[1]
    e = width // SGU_GROUPS
    r = lax.broadcasted_iota(jnp.int32, (SGU_CHUNK, SGU_CHUNK), 0)
    c = lax.broadcasted_iota(jnp.int32, (SGU_CHUNK, SGU_CHUNK), 1)
    n_chunks = tm // SGU_CHUNK
    cat_ref[:, width:] = _rmsnorm(sb_ref[...], gb_ref[...]).astype(BF16)
    va = va_ref[...].astype(BF16)
    mixes = []
    for g in range(SGU_GROUPS):
        w_g = jnp.where(r >= c, ws_ref[g], 0.0).astype(BF16)
        chunks = jnp.concatenate([va[ch * SGU_CHUNK:(ch + 1) * SGU_CHUNK, g * e:(g + 1) * e]
                                  for ch in range(n_chunks)], axis=1)
        mixes.append(jnp.dot(w_g, chunks, preferred_element_type=F32) + bs_ref[:, g:g + 1])
    mix = jnp.concatenate([jnp.concatenate([m[:, ch * e:(ch + 1) * e] for m in mixes], axis=1)
                           for ch in range(n_chunks)], axis=0)
    cat_ref[:, 0:width] = _rmsnorm(u_ref[...] * mix, ga_ref[...]).astype(BF16)
    return x + jnp.dot(cat_ref[...], wo_ref[...], preferred_element_type=F32)


def _merge_body(x_ref, u_ref, va_ref, sb_ref, ws_ref, bs_ref, ga_ref, gb_ref, wo_ref, o_ref, cat_ref):
    o_ref[...] = _merge_rows(x_ref[...], u_ref, va_ref, sb_ref, ws_ref, bs_ref, ga_ref, gb_ref, wo_ref, cat_ref)


def _merge_specs(u, sb, d, va_layer, layer, tm):
    width, sb_width = u.shape[1], sb.shape[1]
    row = lambda i: (i, 0)
    lay3 = lambda i: (layer, 0, 0)
    return [pl.BlockSpec((tm, d), row),
            pl.BlockSpec((tm, width), row),
            pl.BlockSpec((None, tm, width), lambda i: (va_layer, i, 0)),
            pl.BlockSpec((tm, sb_width), row),
            _resident((SGU_GROUPS, SGU_CHUNK, SGU_CHUNK), lambda i: (0, 0, 0)),
            _resident((SGU_CHUNK, SGU_GROUPS), lambda i: (0, 0)),
            _resident((None, 1, width), lay3),
            _resident((None, 1, sb_width), lay3),
            _resident((None, width + sb_width, d), lay3)]


def _merge(x, u, va, va_layer, sb, w_mix, b_mix, g_a, g_b, w_out, layer, tm):
    n, d = x.shape
    return pl.pallas_call(
        _merge_body,
        grid=(n // tm,),
        in_specs=_merge_specs(u, sb, d, va_layer, layer, tm),
        out_specs=pl.BlockSpec((tm, d), lambda i: (i, 0)),
        out_shape=jax.ShapeDtypeStruct((n, d), F32),
        scratch_shapes=[pltpu.VMEM((tm, u.shape[1] + sb.shape[1]), BF16)],
        compiler_params=_params(1),
        name="merge",
    )(x, u, va, sb, w_mix, b_mix, g_a, g_b, w_out)


def _mem_kv_body(m_ref, wk_ref, wv_ref, k_ref, v_ref):
    m = m_ref[...].astype(BF16)
    k_ref[...] = jnp.dot(m, wk_ref[...], preferred_element_type=F32)
    v_ref[...] = jnp.dot(m, wv_ref[...], preferred_element_type=F32)


def _mem_kv(mem, w_mk, w_mv):
    b, m, d = mem.shape
    depth = w_mk.shape[0]
    out = jax.ShapeDtypeStruct((depth, b, m, d), F32)
    w_spec = pl.BlockSpec((None, d, d), lambda l, i: (l, 0, 0))
    o_spec = pl.BlockSpec((None, None, m, d), lambda l, i: (l, i, 0, 0))
    return pl.pallas_call(
        _mem_kv_body,
        grid=(depth, b),
        in_specs=[pl.BlockSpec((None, m, d), lambda l, i: (i, 0, 0)), w_spec, w_spec],
        out_specs=[o_spec, o_spec],
        out_shape=[out, out],
        compiler_params=_params(2),
        name="mem_kv",
    )(mem, w_mk, w_mv)


def _mem_attend_rows(x, ln_ref, wq_ref, mk_ref, mv_ref, wo_ref, oc_ref, n_streams):
    tm, d = x.shape
    dh = d // MEM_HEADS
    rpb = tm // n_streams
    h = _rmsnorm(x, ln_ref[...]).astype(BF16)
    q = (jnp.dot(h, wq_ref[...], preferred_element_type=F32) * dh ** -0.5)
    nt_dims = (((1,), (1,)), ((), ()))
    if MEM_HEADS * rpb <= LANES:
        stacked = MEM_HEADS * rpb
        own = (lax.broadcasted_iota(jnp.int32, (stacked, d), 0) // rpb
               == lax.broadcasted_iota(jnp.int32, (stacked, d), 1) // dh)
        for s in range(n_streams):
            rows = slice(s * rpb, (s + 1) * rpb)
            q_bd = jnp.where(own, jnp.concatenate([q[rows]] * MEM_HEADS, axis=0), 0.0).astype(BF16)
            k_s = jnp.concatenate([mk_ref[s, hd] for hd in range(MEM_HEADS)], axis=1).astype(BF16)
            v_s = jnp.concatenate([mv_ref[s, hd] for hd in range(MEM_HEADS)], axis=1).astype(BF16)
            sc = lax.dot_general(q_bd, k_s, nt_dims, preferred_element_type=F32)
            p = jnp.exp(sc - jnp.max(sc, axis=-1, keepdims=True))
            o = jnp.dot(p.astype(BF16), v_s, preferred_element_type=F32)
            o = jnp.where(own, o / jnp.sum(p, axis=-1, keepdims=True), 0.0)
            o_s = o[0:rpb]
            for hd in range(1, MEM_HEADS):
                o_s = o_s + o[hd * rpb:(hd + 1) * rpb]
            oc_ref[rows, :] = o_s.astype(BF16)
    else:
        q = q.astype(BF16)
        for s in range(n_streams):
            rows = slice(s * rpb, (s + 1) * rpb)
            for hd in range(MEM_HEADS):
                cols = slice(hd * dh, (hd + 1) * dh)
                k = mk_ref[s, :, cols].astype(BF16)
                v = mv_ref[s, :, cols].astype(BF16)
                sc = lax.dot_general(q[rows, cols], k, nt_dims, preferred_element_type=F32)
                p = jnp.exp(sc - jnp.max(sc, axis=-1, keepdims=True))
                o = jnp.dot(p.astype(BF16), v, preferred_element_type=F32) / jnp.sum(p, axis=-1, keepdims=True)
                oc_ref[rows, cols] = o.astype(BF16)
    return x + jnp.dot(oc_ref[...], wo_ref[...], preferred_element_type=F32)


def _mem_attn_body(x_ref, ln_ref, wq_ref, mk_ref, mv_ref, wo_ref, o_ref, oc_ref, *, n_streams):
    o_ref[...] = _mem_attend_rows(x_ref[...], ln_ref, wq_ref, mk_ref, mv_ref, wo_ref, oc_ref, n_streams)


def _mem_specs(mk, d, layer, tm, rows_per_stream):
    n_streams = max(1, tm // rows_per_stream)
    tiles_per_stream = max(1, rows_per_stream // tm)
    lay3 = lambda i: (layer, 0, 0)
    if mk.ndim == 5:
        mem_spec = pl.BlockSpec((None, n_streams) + mk.shape[2:], lambda i: (layer, i // tiles_per_stream, 0, 0, 0))
    else:
        mem_spec = pl.BlockSpec((None, n_streams) + mk.shape[2:], lambda i: (layer, i // tiles_per_stream, 0, 0))
    return n_streams, [_resident((None, 1, d), lay3), _resident((None, d, d), lay3), mem_spec, mem_spec,
                       _resident((None, d, d), lay3)]


def _mem_attn(x, ln, w_mq, mk, mv, w_mo, layer, tm, rows_per_stream):
    n, d = x.shape
    row = lambda i: (i, 0)
    n_streams, specs = _mem_specs(mk, d, layer, tm, rows_per_stream)
    return pl.pallas_call(
        functools.partial(_mem_attn_body, n_streams=n_streams),
        grid=(n // tm,),
        in_specs=[pl.BlockSpec((tm, d), row)] + specs,
        out_specs=pl.BlockSpec((tm, d), row),
        out_shape=jax.ShapeDtypeStruct((n, d), F32),
        scratch_shapes=[pltpu.VMEM((tm, d), BF16)],
        compiler_params=_params(1),
        name="mem_attn",
    )(x, ln, w_mq, mk, mv, w_mo)


def _merge_mem_body(x_ref, u_ref, va_ref, sb_ref, ws_ref, bs_ref, ga_ref, gb_ref, wo_ref,
                    ln_ref, wq_ref, mk_ref, mv_ref, wmo_ref, o_ref, cat_ref, oc_ref, *, n_streams):
    x = _merge_rows(x_ref[...], u_ref, va_ref, sb_ref, ws_ref, bs_ref, ga_ref, gb_ref, wo_ref, cat_ref)
    o_ref[...] = _mem_attend_rows(x, ln_ref, wq_ref, mk_ref, mv_ref, wmo_ref, oc_ref, n_streams)


def _merge_mem(x, u, va, va_layer, sb, w_mix, b_mix, g_a, g_b, w_out, ln, w_mq, mk, mv, w_mo, layer, tm,
               rows_per_stream):
    n, d = x.shape
    n_streams, mem_specs = _mem_specs(mk, d, layer, tm, rows_per_stream)
    return pl.pallas_call(
        functools.partial(_merge_mem_body, n_streams=n_streams),
        grid=(n // tm,),
        in_specs=_merge_specs(u, sb, d, va_layer, layer, tm) + mem_specs,
        out_specs=pl.BlockSpec((tm, d), lambda i: (i, 0)),
        out_shape=jax.ShapeDtypeStruct((n, d), F32),
        scratch_shapes=[pltpu.VMEM((tm, u.shape[1] + sb.shape[1]), BF16), pltpu.VMEM((tm, d), BF16)],
        compiler_params=_params(1),
        name="merge_mem",
    )(x, u, va, sb, w_mix, b_mix, g_a, g_b, w_out, ln, w_mq, mk, mv, w_mo)


def _ffn_body(x_ref, ln_ref, wg_ref, wu_ref, wd_ref, lnf_ref, o_ref, *, final_norm):
    x = x_ref[...]
    h = _rmsnorm(x, ln_ref[...]).astype(BF16)
    gate = jnp.dot(h, wg_ref[...], preferred_element_type=F32)
    up = jnp.dot(h, wu_ref[...], preferred_element_type=F32)
    act = (jax.nn.silu(gate) * up).astype(BF16)
    y = x + jnp.dot(act, wd_ref[...], preferred_element_type=F32)
    if final_norm:
        y = _rmsnorm(y, lnf_ref[...])
    o_ref[...] = y


def _ffn(x, ln, w_gate, w_up, w_down, ln_final, layer, tm, final_norm):
    n, d = x.shape
    d_ff = w_gate.shape[-1]
    row = lambda i: (i, 0)
    lay3 = lambda i: (layer, 0, 0)
    return pl.pallas_call(
        functools.partial(_ffn_body, final_norm=final_norm),
        grid=(n // tm,),
        in_specs=[pl.BlockSpec((tm, d), row),
                  _resident((None, 1, d), lay3),
                  _resident((None, d, d_ff), lay3),
                  _resident((None, d, d_ff), lay3),
                  _resident((None, d_ff, d), lay3),
                  _resident((1, d), lambda i: (0, 0))],
        out_specs=pl.BlockSpec((tm, d), row),
        out_shape=jax.ShapeDtypeStruct((n, d), F32),
        compiler_params=_params(1),
        name="ffn",
    )(x, ln, w_gate, w_up, w_down, ln_final)


def kernel(x_prompt, x_sample, cache_sb_k, cache_sb_v, cache_mem_k, cache_mem_v, mem_prompt, ln_mix, w_in, g_sgu_v, w_sgu, b_sgu, g_out_sgu, g_out_sb, w_out, ln_mem, w_mq, w_mk, w_mv, w_mo, ln_ffn, w_ffn_gate, w_ffn_up, w_ffn_down, ln_final):
    batch, t_len, d = x_prompt.shape
    bs, n_new, _ = x_sample.shape
    depth = w_in.shape[0]
    past = cache_sb_k.shape[2]
    n_mem = mem_prompt.shape[1]
    width = d // 2
    dh = width // SB_HEADS
    e = width // SGU_GROUPS
    assert t_len % ROW_TILE == 0 and t_len % SB_TILE == 0 and SGU_CHUNK % n_new == 0

    w_in_b, w_out_b = w_in.astype(BF16), w_out.astype(BF16)
    w_kvt_b = jnp.swapaxes(w_in_b[:, :, 3 * width:], 1, 2)
    w_mq_b, w_mk_b, w_mv_b, w_mo_b = (w.astype(BF16) for w in (w_mq, w_mk, w_mv, w_mo))
    w_g_b, w_u_b, w_d_b = (w.astype(BF16) for w in (w_ffn_gate, w_ffn_up, w_ffn_down))
    as_rows = lambda g: g.reshape(depth, 1, -1)
    ln_mix_r, ln_mem_r, ln_ffn_r = as_rows(ln_mix), as_rows(ln_mem), as_rows(ln_ffn)
    g_a_r, g_b_r = as_rows(g_out_sgu), as_rows(g_out_sb)
    ln_final_r = ln_final.reshape(1, d)

    cache_kt = cache_sb_k.transpose(0, 1, 3, 4, 2).reshape(depth, bs, width, past)
    cache_vt = cache_sb_v.transpose(0, 1, 3, 4, 2).reshape(depth, bs, width, past)
    cmem_k = cache_mem_k.astype(BF16).transpose(0, 1, 3, 2, 4)
    cmem_v = cache_mem_v.astype(BF16).transpose(0, 1, 3, 2, 4)

    mk_p, mv_p = _mem_kv(mem_prompt, w_mk_b, w_mv_b)

    n_p = batch * t_len
    x = x_prompt.reshape(n_p, d)
    kt_st = jnp.zeros((depth, batch, width, t_len), F32)
    vt_st = jnp.zeros((depth, batch, width, t_len), F32)
    for l in range(depth):
        u, va, qs, ktb, vtb, kt_st, vt_st = _in_proj(x, ln_mix_r, w_in_b, w_kvt_b, g_sgu_v, kt_st, vt_st, None, l, ROW_TILE_WIDE)
        sb = _sb_prompt(qs, ktb, vtb, batch, t_len)
        x = _merge_mem(x, u, va, 0, sb, w_sgu[l], b_sgu[l].T, g_a_r, g_b_r, w_out_b,
                       ln_mem_r, w_mq_b, mk_p, mv_p, w_mo_b, l, ROW_TILE, t_len)
        x = _ffn(x, ln_ffn_r, w_g_b, w_u_b, w_d_b, ln_final_r, l, ROW_TILE, l == depth - 1)
    y_prompt = x.reshape(batch, t_len, d)
    sb_k_prompt = kt_st.reshape(depth, batch, SB_HEADS, dh, t_len).transpose(0, 1, 4, 2, 3)
    sb_v_prompt = vt_st.reshape(depth, batch, SB_HEADS, dh, t_len).transpose(0, 1, 4, 2, 3)

    n_s = bs * n_new
    tm_s = min(ROW_TILE, n_s)
    tm_mem = 8 * n_new
    reps = SGU_CHUNK // n_new
    x = x_sample.reshape(n_s, d)
    k_st = jnp.zeros((depth, n_s, width), F32)
    v_st = jnp.zeros((depth, n_s, width), F32)
    va_st = jnp.zeros((depth, n_s, width), F32)
    for l in range(depth):
        u, va_st, qs, kb, vb, k_st, v_st = _in_proj(x, ln_mix_r, w_in_b, None, g_sgu_v, k_st, v_st, va_st, l, tm_s)
        sb = _sb_sample(qs, kb, vb, cache_kt, cache_vt, l, n_new)
        w_blk = jnp.einsum('ab,gts->gatbs', jnp.eye(reps, dtype=F32), w_sgu[l][:, :n_new, :n_new])
        w_blk = w_blk.reshape(SGU_GROUPS, SGU_CHUNK, SGU_CHUNK)
        b_blk = jnp.tile(b_sgu[l][:, :n_new], (1, reps)).T
        x = _merge(x, u, va_st, l, sb, w_blk, b_blk, g_a_r, g_b_r, w_out_b, l, tm_s)
        x = _mem_attn(x, ln_mem_r, w_mq_b, cmem_k, cmem_v, w_mo_b, l, tm_mem, n_new)
        x = _ffn(x, ln_ffn_r, w_g_b, w_u_b, w_d_b, ln_final_r, l, tm_s, l == depth - 1)
    y_sample = x.reshape(bs, n_new, d)

    return (y_prompt, y_sample, sb_k_prompt, sb_v_prompt,
            mk_p.reshape(depth, batch, n_mem, MEM_HEADS, d // MEM_HEADS),
            mv_p.reshape(depth, batch, n_mem, MEM_HEADS, d // MEM_HEADS),
            k_st.reshape(depth, bs, n_new, SB_HEADS, dh),
            v_st.reshape(depth, bs, n_new, SB_HEADS, dh),
            va_st.reshape(depth, bs, n_new, SGU_GROUPS, e))
```
